```python
import math, functools
import jax, jax.numpy as jnp
from jax import lax
import numpy as np

D_MODEL = 2048
BATCH = 8
SEQ = 2048
DEPTH = 1
DEC_BATCH = 32
DEC_SEQ = 1
PAST_LEN = 8192
PAGE_SIZE = 128

CHUNK = 128
A_GROUPS = 8
A_GROUP_CH = D_MODEL // 16
A_WIDTH = A_GROUPS * A_GROUP_CH
SB_HEADS = 8
SB_HEAD_DIM = D_MODEL // 16
SB_WIDTH = SB_HEADS * SB_HEAD_DIM
Q_BLOCK = 128
SB_BIAS_INIT = -6.0
D_FF = ((8 * D_MODEL // 3 + 255) // 256) * 256
EPS = 1e-6
N_IN = 2 * A_WIDTH + 3 * SB_WIDTH + 2 * D_MODEL
IN_SPLITS = (A_WIDTH, 2 * A_WIDTH, 2 * A_WIDTH + SB_WIDTH, 2 * A_WIDTH + 2 * SB_WIDTH,
             2 * A_WIDTH + 3 * SB_WIDTH, 2 * A_WIDTH + 3 * SB_WIDTH + D_MODEL)

kernel_name = "hybrid_gmlp_stickbreaking_decode_step"


def rms_norm(x, g):
    xf = x.astype(jnp.float32)
    y = xf * lax.rsqrt(jnp.mean(xf * xf, axis=-1, keepdims=True) + EPS)
    return (y * g.astype(jnp.float32)).astype(x.dtype)


def layer_norm(x, g, b):
    xf = x.astype(jnp.float32)
    mu = jnp.mean(xf, axis=-1, keepdims=True)
    xc = xf - mu
    var = jnp.mean(xc * xc, axis=-1, keepdims=True)
    y = xc * lax.rsqrt(var + EPS) * g.astype(jnp.float32) + b.astype(jnp.float32)
    return y.astype(x.dtype)


def stick_breaking(q, k, v, bias, q_pos, k_pos):
    d = q.shape[-1]
    z = jnp.einsum('bqhd,bkhd->bhqk', q, k).astype(jnp.float32) * (1.0 / math.sqrt(d))
    z = z + bias.astype(jnp.float32)[None, :, None, None]
    causal = k_pos[None, :] < q_pos[:, None]
    log_beta = jax.nn.log_sigmoid(z)
    log_1m = jnp.where(causal, jax.nn.log_sigmoid(-z), 0.0)
    later = lax.cumsum(log_1m, axis=3, reverse=True) - log_1m
    w = jnp.where(causal, jnp.exp(log_beta + later), 0.0)
    return jnp.einsum('bhqk,bkhd->bqhd', w.astype(v.dtype), v)


def sb_prompt(q, k, v, bias):
    B, S, H, d = q.shape
    nb = S // Q_BLOCK
    pos = jnp.arange(S, dtype=jnp.int32)
    qb = jnp.moveaxis(q.reshape(B, nb, Q_BLOCK, H, d), 1, 0)
    pb = pos.reshape(nb, Q_BLOCK)
    ob = lax.map(lambda blk: stick_breaking(blk[0], k, v, bias, blk[1], pos), (qb, pb))
    return jnp.moveaxis(ob, 0, 1).reshape(B, S, H, d)


def sb_sample(q, k, v, bias, cache_k, cache_v, page_table):
    DB, T, H, d = q.shape
    past = page_table.shape[1] * cache_k.shape[1]
    k_past = cache_k[page_table].reshape(DB, past, H, d)
    v_past = cache_v[page_table].reshape(DB, past, H, d)
    k_all = jnp.concatenate([k_past, k.astype(k_past.dtype)], axis=1)
    v_all = jnp.concatenate([v_past, v.astype(v_past.dtype)], axis=1)
    k_pos = jnp.arange(past + T, dtype=jnp.int32)
    q_pos = past + jnp.arange(T, dtype=jnp.int32)
    return stick_breaking(q, k_all, v_all, bias, q_pos, k_pos)


def chunk_mlp_prompt(u, v, w_s, b_s):
    B, S, _ = v.shape
    nc = S // CHUNK
    ws = jnp.where(jnp.tril(jnp.ones((CHUNK, CHUNK), dtype=bool)), w_s, 0)
    vc = v.reshape(B, nc, CHUNK, A_GROUPS, A_GROUP_CH)
    s = jnp.einsum('gts,bnsgc->bntgc', ws, vc) + jnp.swapaxes(b_s, 0, 1)[:, :, None]
    return u * s.reshape(B, S, A_WIDTH)


def chunk_mlp_sample(u, v, w_s, b_s):
    DB, T, _ = v.shape
    ws = jnp.where(jnp.tril(jnp.ones((T, T), dtype=bool)), w_s[:, :T, :T], 0)
    s = jnp.einsum('gts,bsgc->btgc', ws, v.reshape(DB, T, A_GROUPS, A_GROUP_CH))
    s = s + jnp.swapaxes(b_s[:, :T], 0, 1)[:, :, None]
    return u * s.reshape(DB, T, A_WIDTH)


def decoder_layer(x, c, sb_fn, chunk_fn, w_ada, b_ada, g_pre_mix, g_post_mix, w_in, ln_v_g, ln_v_b,
                  w_s, b_s, sb_bias, w_branch_a, w_branch_b, w_out, g_pre_ffn, g_post_ffn,
                  w_ffn_in, w_ffn_out):
    B, T, _ = x.shape
    mods = jax.nn.silu(c) @ w_ada + b_ada
    sh1, sc1, gt1, sh2, sc2, gt2 = [m[:, None, :] for m in jnp.split(mods, 6, axis=-1)]
    h = rms_norm(x, g_pre_mix) * (1 + sc1) + sh1
    proj = h @ w_in
    u_a, v_a, q, k, v_b, gate_a, gate_b = jnp.split(proj, IN_SPLITS, axis=-1)
    u_a = jax.nn.gelu(u_a)
    v_a = layer_norm(jax.nn.gelu(v_a), ln_v_g, ln_v_b)
    a_out = chunk_fn(u_a, v_a, w_s, b_s)
    q = q.reshape(B, T, SB_HEADS, SB_HEAD_DIM)
    k = k.reshape(B, T, SB_HEADS, SB_HEAD_DIM)
    v_b = v_b.reshape(B, T, SB_HEADS, SB_HEAD_DIM)
    b_out = sb_fn(q, k, v_b, sb_bias).reshape(B, T, SB_WIDTH)
    merged = jax.nn.sigmoid(gate_a) * (a_out @ w_branch_a) + jax.nn.sigmoid(gate_b) * (b_out @ w_branch_b)
    x = x + gt1 * rms_norm(merged @ w_out, g_post_mix)
    h2 = rms_norm(x, g_pre_ffn) * (1 + sc2) + sh2
    f_gate, f_up = jnp.split(h2 @ w_ffn_in, 2, axis=-1)
    f = (jax.nn.silu(f_gate) * f_up) @ w_ffn_out
    x = x + gt2 * rms_norm(f, g_post_ffn)
    return x, k, v_b, v_a


def setup_inputs(seed: int = 0) -> dict:
    key = jax.random.key(seed)
    ks = jax.random.split(key, 24)
    n_pages = PAST_LEN // PAGE_SIZE
    n_used = DEC_BATCH * n_pages
    n_pool = n_used + max(1, n_used // 4)

    def nrm(k, shape, scale):
        return jax.random.normal(k, shape, jnp.float32) * scale

    perm = jax.random.permutation(ks[6], n_pool)[:n_used]
    page_table = perm.reshape(DEC_BATCH, n_pages).astype(jnp.int32)
    Dm = D_MODEL
    return {
        "x_prompt": nrm(ks[0], (BATCH, SEQ, Dm), 1.0),
        "x_sample": nrm(ks[1], (DEC_BATCH, DEC_SEQ, Dm), 1.0),
        "c_prompt": nrm(ks[2], (BATCH, Dm), 1.0),
        "c_sample": nrm(ks[3], (DEC_BATCH, Dm), 1.0),
        "cache_k": nrm(ks[4], (DEPTH, n_pool, PAGE_SIZE, SB_HEADS, SB_HEAD_DIM), 1.0),
        "cache_v": nrm(ks[5], (DEPTH, n_pool, PAGE_SIZE, SB_HEADS, SB_HEAD_DIM), 1.0),
        "page_table": page_table,
        "w_ada": nrm(ks[7], (DEPTH, Dm, 6 * Dm), 0.5 * Dm ** -0.5),
        "b_ada": nrm(ks[8], (DEPTH, 6 * Dm), 0.01),
        "g_pre_mix": 1.0 + nrm(ks[9], (DEPTH, Dm), 0.02),
        "g_post_mix": 1.0 + nrm(ks[10], (DEPTH, Dm), 0.02),
        "w_in": nrm(ks[11], (DEPTH, Dm, N_IN), Dm ** -0.5),
        "ln_v_g": 1.0 + nrm(ks[12], (DEPTH, A_WIDTH), 0.02),
        "ln_v_b": nrm(ks[13], (DEPTH, A_WIDTH), 0.01),
        "w_s": nrm(ks[14], (DEPTH, A_GROUPS, CHUNK, CHUNK), CHUNK ** -0.5),
        "b_s": 1.0 + nrm(ks[15], (DEPTH, A_GROUPS, CHUNK), 0.02),
        "sb_bias": SB_BIAS_INIT + nrm(ks[23], (DEPTH, SB_HEADS), 0.5),
        "w_branch_a": nrm(ks[16], (DEPTH, A_WIDTH, Dm), A_WIDTH ** -0.5),
        "w_branch_b": nrm(ks[17], (DEPTH, SB_WIDTH, Dm), SB_WIDTH ** -0.5),
        "w_out": nrm(ks[18], (DEPTH, Dm, Dm), Dm ** -0.5),
        "g_pre_ffn": 1.0 + nrm(ks[19], (DEPTH, Dm), 0.02),
        "g_post_ffn": 1.0 + nrm(ks[20], (DEPTH, Dm), 0.02),
        "w_ffn_in": nrm(ks[21], (DEPTH, Dm, 2 * D_FF), Dm ** -0.5),
        "w_ffn_out": nrm(ks[22], (DEPTH, D_FF, Dm), D_FF ** -0.5),
    }


def reference(x_prompt, x_sample, c_prompt, c_sample, cache_k, cache_v, page_table,
              w_ada, b_ada, g_pre_mix, g_post_mix, w_in, ln_v_g, ln_v_b, w_s, b_s, sb_bias,
              w_branch_a, w_branch_b, w_out, g_pre_ffn, g_post_ffn, w_ffn_in, w_ffn_out):
    yp, ys = x_prompt, x_sample
    kp_l, vp_l, ks_l, vs_l, cv_l = [], [], [], [], []
    for l in range(DEPTH):
        lw = (w_ada[l], b_ada[l], g_pre_mix[l], g_post_mix[l], w_in[l], ln_v_g[l], ln_v_b[l],
              w_s[l], b_s[l], sb_bias[l], w_branch_a[l], w_branch_b[l], w_out[l], g_pre_ffn[l],
              g_post_ffn[l], w_ffn_in[l], w_ffn_out[l])
        yp, kp, vp, _ = decoder_layer(yp, c_prompt, sb_prompt, chunk_mlp_prompt, *lw)
        sb_fn = functools.partial(sb_sample, cache_k=cache_k[l], cache_v=cache_v[l], page_table=page_table)
        ys, k_s, v_s, cv_s = decoder_layer(ys, c_sample, sb_fn, chunk_mlp_sample, *lw)
        kp_l.append(kp)
        vp_l.append(vp)
        ks_l.append(k_s)
        vs_l.append(v_s)
        cv_l.append(cv_s)
    k_prompt = jnp.stack(kp_l)
    v_prompt = jnp.stack(vp_l)
    k_sample = jnp.stack(ks_l)
    v_sample = jnp.stack(vs_l)
    chunk_v_sample = jnp.stack(cv_l)
    return (yp, ys, k_prompt, v_prompt, k_sample, v_sample, chunk_v_sample)
```

```python
import functools
import math

import jax
import jax.numpy as jnp
from jax import lax
from jax.experimental import pallas as pl
from jax.experimental.pallas import tpu as pltpu

F32 = jnp.float32
BF16 = jnp.bfloat16

EPS = 1e-6
LANE = 128
HEAD_DIM = 128
N_HEADS = 8
A_GROUPS = 8
CHUNK = 128
A_WIDTH = A_GROUPS * CHUNK
SB_WIDTH = N_HEADS * HEAD_DIM
COL_TILE = 1024
VMEM_LIMIT = 56 * 1024 * 1024


def _params(semantics):
    return pltpu.CompilerParams(dimension_semantics=semantics, vmem_limit_bytes=VMEM_LIMIT)


def _rms(x, g):
    return x * lax.rsqrt(jnp.mean(x * x, axis=-1, keepdims=True) + EPS) * g


def _sigmoid(x):
    return 1.0 / (1.0 + jnp.exp(-x))


def _ada_kernel(c_ref, w_ref, b_ref, o_ref):
    c = c_ref[...]
    a = (c * _sigmoid(c)).astype(BF16)
    o_ref[...] = jnp.dot(a, w_ref[...].astype(BF16), preferred_element_type=F32) + b_ref[...]


def _ada(c_all, w_ada, b_ada):
    rows, d = c_all.shape
    n = w_ada.shape[1]
    tn = 1024
    return pl.pallas_call(
        _ada_kernel,
        grid=(n // tn,),
        in_specs=[
            pl.BlockSpec((rows, d), lambda j: (0, 0)),
            pl.BlockSpec((d, tn), lambda j: (0, j)),
            pl.BlockSpec((1, tn), lambda j: (0, j)),
        ],
        out_specs=pl.BlockSpec((rows, tn), lambda j: (0, j)),
        out_shape=jax.ShapeDtypeStruct((rows, n), F32),
        compiler_params=_params(("arbitrary",)),
    )(c_all, w_ada, b_ada.reshape(1, n))


def _inproj_kernel(x_ref, sc_ref, sh_ref, g_ref, w_ref, lng_ref, lnb_ref,
                   proj_ref, k_ref, v_ref, va_ref, h_scr):
    j = pl.program_id(1)

    @pl.when(j == 0)
    def _():
        h = _rms(x_ref[...], g_ref[...]) * (1.0 + sc_ref[0]) + sh_ref[0]
        h_scr[...] = h.astype(BF16)

    acc = jnp.dot(h_scr[...], w_ref[...], preferred_element_type=F32)

    @pl.when(j == 0)
    def _():
        proj_ref[...] = jax.nn.gelu(acc).astype(BF16)

    @pl.when(j == 1)
    def _():
        g = jax.nn.gelu(acc)
        xc = g - jnp.mean(g, axis=-1, keepdims=True)
        var = jnp.mean(xc * xc, axis=-1, keepdims=True)
        y = xc * lax.rsqrt(var + EPS) * lng_ref[...] + lnb_ref[...]
        va_ref[...] = y
        proj_ref[...] = y.astype(BF16)

    @pl.when(j == 2)
    def _():
        proj_ref[...] = acc.astype(BF16)

    @pl.when(j == 3)
    def _():
        k_ref[...] = acc
        proj_ref[...] = acc.astype(BF16)

    @pl.when(j == 4)
    def _():
        v_ref[...] = acc
        proj_ref[...] = acc.astype(BF16)

    @pl.when(j >= 5)
    def _():
        proj_ref[...] = _sigmoid(acc).astype(BF16)


def _inproj(x, sc, sh, g_pre, w_in, ln_g, ln_b, bm):
    m, d = x.shape
    n_in = w_in.shape[1]
    tiles_per_group = m // bm // sc.shape[0]
    r = sc.shape[1]
    mod_spec = pl.BlockSpec((1, r, d), lambda i, j: (i // tiles_per_group, 0, 0))
    row_spec = pl.BlockSpec((bm, COL_TILE), lambda i, j: (i, 0))
    return pl.pallas_call(
        _inproj_kernel,
        grid=(m // bm, n_in // COL_TILE),
        in_specs=[
            pl.BlockSpec((bm, d), lambda i, j: (i, 0)),
            mod_spec, mod_spec,
            pl.BlockSpec((1, d), lambda i, j: (0, 0)),
            pl.BlockSpec((d, COL_TILE), lambda i, j: (0, j)),
            pl.BlockSpec((1, A_WIDTH), lambda i, j: (0, 0)),
            pl.BlockSpec((1, A_WIDTH), lambda i, j: (0, 0)),
        ],
        out_specs=[
            pl.BlockSpec((bm, COL_TILE), lambda i, j: (i, j)),
            row_spec, row_spec, row_spec,
        ],
        out_shape=[
            jax.ShapeDtypeStruct((m, n_in), BF16),
            jax.ShapeDtypeStruct((m, SB_WIDTH), F32),
            jax.ShapeDtypeStruct((m, SB_WIDTH), F32),
            jax.ShapeDtypeStruct((m, A_WIDTH), F32),
        ],
        scratch_shapes=[pltpu.VMEM((bm, d), BF16)],
        compiler_params=_params(("arbitrary", "arbitrary")),
    )(x, sc, sh, g_pre.reshape(1, d), w_in, ln_g.reshape(1, A_WIDTH), ln_b.reshape(1, A_WIDTH))


def _log_terms(z):
    soft = jnp.log1p(jnp.exp(-jnp.abs(z)))
    log_beta = jnp.minimum(z, 0.0) - soft
    return log_beta, log_beta - z


def _later_sum(x, tri):
    hi = x.astype(BF16)
    lo = (x - hi.astype(F32)).astype(BF16)
    return (jnp.dot(hi, tri, preferred_element_type=F32)
            + jnp.dot(lo, tri, preferred_element_type=F32))


def _strict_lower(n):
    row = lax.broadcasted_iota(jnp.int32, (n, n), 0)
    col = lax.broadcasted_iota(jnp.int32, (n, n), 1)
    return (row > col).astype(BF16)


def _attn_block(q, k, v, bias, tri, run, acc, causal):
    s = lax.dot_general(q, k, (((1,), (1,)), ((), ())), preferred_element_type=F32)
    z = s * (1.0 / math.sqrt(HEAD_DIM)) + bias
    log_beta, log_1m = _log_terms(z)
    if causal is not None:
        log_1m = jnp.where(causal, log_1m, 0.0)
    excl = _later_sum(log_1m, tri)
    w = jnp.exp(log_beta + excl + run)
    if causal is not None:
        w = jnp.where(causal, w, 0.0)
    acc = acc + jnp.dot(w.astype(BF16), v, preferred_element_type=F32)
    run = run + excl[:, :1] + log_1m[:, :1]
    return run, acc


def _attn_kernel(bias_ref, q_ref, k_ref, v_ref, o_ref, *, blk):
    h = pl.program_id(1)
    qi = pl.program_id(2)
    bias = bias_ref[h]
    q = q_ref[...]
    tri = _strict_lower(blk)
    row = lax.broadcasted_iota(jnp.int32, (blk, blk), 0)
    col = lax.broadcasted_iota(jnp.int32, (blk, blk), 1)
    causal = col < row

    def kv(kb):
        start = pl.multiple_of(kb * blk, blk)
        return k_ref[pl.ds(start, blk), :], v_ref[pl.ds(start, blk), :]

    k, v = kv(qi)
    run = jnp.zeros((blk, 1), F32)
    acc = jnp.zeros((blk, HEAD_DIM), F32)
    run, acc = _attn_block(q, k, v, bias, tri, run, acc, causal)

    def body(i, carry):
        k, v = kv(qi - i)
        return _attn_block(q, k, v, bias, tri, carry[0], carry[1], None)

    run, acc = lax.fori_loop(1, qi + 1, body, (run, acc))
    o_ref[...] = acc.astype(BF16)


def _attn_prompt(proj, sb_bias, batch, seq, blk):
    nq = seq // blk
    q_col, k_col, v_col = (2 * A_WIDTH) // LANE, (2 * A_WIDTH + SB_WIDTH) // LANE, (2 * A_WIDTH + 2 * SB_WIDTH) // LANE
    return pl.pallas_call(
        functools.partial(_attn_kernel, blk=blk),
        grid=(batch, N_HEADS, nq),
        in_specs=[
            pl.BlockSpec(memory_space=pltpu.SMEM),
            pl.BlockSpec((blk, HEAD_DIM), lambda b, h, i: (b * nq + i, q_col + h)),
            pl.BlockSpec((seq, HEAD_DIM), lambda b, h, i: (b, k_col + h)),
            pl.BlockSpec((seq, HEAD_DIM), lambda b, h, i: (b, v_col + h)),
        ],
        out_specs=pl.BlockSpec((blk, HEAD_DIM), lambda b, h, i: (b * nq + i, h)),
        out_shape=jax.ShapeDtypeStruct((batch * seq, SB_WIDTH), BF16),
        compiler_params=_params(("arbitrary", "arbitrary", "arbitrary")),
    )(sb_bias, proj, proj, proj)


def _attn_sample_kernel(pt_ref, q_ref, bias_ref, *refs, pages_per_step):
    k_refs = refs[:pages_per_step]
    v_refs = refs[pages_per_step:2 * pages_per_step]
    o_ref, run_scr, acc_scr = refs[2 * pages_per_step:]
    step = pl.program_id(1)
    page_cols = k_refs[0].shape[1]
    n_chunks = page_cols // LANE

    @pl.when(step == 0)
    def _():
        run_scr[...] = jnp.zeros_like(run_scr)
        acc_scr[...] = jnp.zeros_like(acc_scr)

    q = q_ref[0].astype(BF16)
    bias = bias_ref[...]
    tri = _strict_lower(LANE)
    head = lax.broadcasted_iota(jnp.int32, (N_HEADS, page_cols), 0)
    col = lax.broadcasted_iota(jnp.int32, (N_HEADS, page_cols), 1)
    own = (col % N_HEADS) == head

    run = run_scr[...]
    acc = acc_scr[...]
    for r in range(pages_per_step):
        k = k_refs[r][0].astype(BF16)
        v = v_refs[r][0].astype(BF16)
        s = lax.dot_general(q, k, (((1,), (1,)), ((), ())), preferred_element_type=F32)
        z = s * (1.0 / math.sqrt(HEAD_DIM)) + bias
        log_beta, log_1m = _log_terms(z)
        log_1m = jnp.where(own, log_1m, 0.0)
        chunks = [log_1m[:, c * LANE:(c + 1) * LANE] for c in range(n_chunks)]
        excl = _later_sum(jnp.concatenate(chunks, axis=0), tri)
        later = [None] * n_chunks
        for c in reversed(range(n_chunks)):
            e = excl[c * N_HEADS:(c + 1) * N_HEADS]
            later[c] = e + run
            run = run + e[:, :1] + chunks[c][:, :1]
        w = jnp.exp(log_beta + jnp.concatenate(later, axis=1))
        w = jnp.where(own, w, 0.0).astype(BF16)
        acc = acc + jnp.dot(w, v, preferred_element_type=F32)
    run_scr[...] = run
    acc_scr[...] = acc

    @pl.when(step == pl.num_programs(1) - 1)
    def _():
        o_ref[0] = acc.astype(BF16)


def _attn_sample(q, sb_bias, cache_k, cache_v, layer, page_table, pages_per_step):
    db = q.shape[0]
    depth, n_pool, page, heads, d = cache_k.shape
    n_pages = page_table.shape[1]
    steps = n_pages // pages_per_step
    ck = cache_k.reshape(depth * n_pool, page * heads, d)
    cv = cache_v.reshape(depth * n_pool, page * heads, d)

    def page_spec(r):
        def index(b, s, pt):
            return (layer * n_pool + pt[b, n_pages - 1 - (s * pages_per_step + r)], 0, 0)
        return pl.BlockSpec((1, page * heads, d), index)

    grid_spec = pltpu.PrefetchScalarGridSpec(
        num_scalar_prefetch=1,
        grid=(db, steps),
        in_specs=[
            pl.BlockSpec((1, heads, d), lambda b, s, pt: (b, 0, 0)),
            pl.BlockSpec((heads, 1), lambda b, s, pt: (0, 0)),
        ] + [page_spec(r) for r in range(pages_per_step)] * 2,
        out_specs=pl.BlockSpec((1, heads, d), lambda b, s, pt: (b, 0, 0)),
        scratch_shapes=[pltpu.VMEM((heads, 1), F32), pltpu.VMEM((heads, d), F32)],
    )
    return pl.pallas_call(
        functools.partial(_attn_sample_kernel, pages_per_step=pages_per_step),
        grid_spec=grid_spec,
        out_shape=jax.ShapeDtypeStruct((db, heads, d), BF16),
        compiler_params=_params(("arbitrary", "arbitrary")),
    )(page_table, q, sb_bias.reshape(heads, 1),
      *([ck] * pages_per_step), *([cv] * pages_per_step))


def _mix_kernel(u_ref, va_ref, b_ref, ga0_ref, ga1_ref, gb0_ref, gb1_ref, x_ref, gt_ref, gpost_ref,
                ws_ref, bs_ref, wa_ref, wb_ref, wo_ref, o_ref, *, chunked):
    bm = u_ref.shape[0]
    u = u_ref[...].astype(F32)
    if chunked:
        row = lax.broadcasted_iota(jnp.int32, (CHUNK, CHUNK), 0)
        col = lax.broadcasted_iota(jnp.int32, (CHUNK, CHUNK), 1)
        keep = col <= row
        cols = []
        for g in range(A_GROUPS):
            w = jnp.where(keep, ws_ref[g], 0.0).astype(BF16)
            b = bs_ref[g]
            rows = []
            for c in range(bm // CHUNK):
                v = va_ref[c * CHUNK:(c + 1) * CHUNK, g * CHUNK:(g + 1) * CHUNK]
                rows.append(jnp.dot(w, v, preferred_element_type=F32) + b)
            cols.append(jnp.concatenate(rows, axis=0))
        s = jnp.concatenate(cols, axis=1)
    else:
        s = va_ref[...].astype(F32) * ws_ref[...] + bs_ref[...]
    a = (u * s).astype(BF16)
    p = jnp.dot(a, wa_ref[...], preferred_element_type=F32)
    q = jnp.dot(b_ref[...], wb_ref[...], preferred_element_type=F32)
    half = p.shape[1] // 2
    merged = jnp.concatenate([
        ga0_ref[...].astype(F32) * p[:, :half] + gb0_ref[...].astype(F32) * q[:, :half],
        ga1_ref[...].astype(F32) * p[:, half:] + gb1_ref[...].astype(F32) * q[:, half:],
    ], axis=1).astype(BF16)
    mix = jnp.dot(merged, wo_ref[...], preferred_element_type=F32)
    o_ref[...] = x_ref[...] + gt_ref[0] * _rms(mix, gpost_ref[...])


def _mix(proj, b_out, x, gt, g_post, ws, bs, wa, wb, wo, bm, chunked):
    m, d = x.shape
    tiles_per_group = m // bm // gt.shape[0]
    r = gt.shape[1]

    def col_spec(c):
        return pl.BlockSpec((bm, COL_TILE), lambda i: (i, c))

    def whole(a):
        return pl.BlockSpec(a.shape, lambda i: (0,) * a.ndim)

    return pl.pallas_call(
        functools.partial(_mix_kernel, chunked=chunked),
        grid=(m // bm,),
        in_specs=[
            col_spec(0), col_spec(1),
            pl.BlockSpec((bm, SB_WIDTH), lambda i: (i, 0)),
            col_spec(5), col_spec(6), col_spec(7), col_spec(8),
            pl.BlockSpec((bm, d), lambda i: (i, 0)),
            pl.BlockSpec((1, r, d), lambda i: (i // tiles_per_group, 0, 0)),
            pl.BlockSpec((1, d), lambda i: (0, 0)),
            whole(ws), whole(bs), whole(wa), whole(wb), whole(wo),
        ],
        out_specs=pl.BlockSpec((bm, d), lambda i: (i, 0)),
        out_shape=jax.ShapeDtypeStruct((m, d), F32),
        compiler_params=_params(("arbitrary",)),
    )(proj, proj, b_out, proj, proj, proj, proj, x, gt, g_post.reshape(1, d), ws, bs, wa, wb, wo)


def _ffn_kernel(x_ref, sc_ref, sh_ref, gt_ref, gpre_ref, gpost_ref, wg_ref, wu_ref, wo_ref, o_ref,
                h_scr, acc_scr):
    j = pl.program_id(1)

    @pl.when(j == 0)
    def _():
        h = _rms(x_ref[...], gpre_ref[...]) * (1.0 + sc_ref[0]) + sh_ref[0]
        h_scr[...] = h.astype(BF16)
        acc_scr[...] = jnp.zeros_like(acc_scr)

    h = h_scr[...]
    gate = jnp.dot(h, wg_ref[...], preferred_element_type=F32)
    up = jnp.dot(h, wu_ref[...], preferred_element_type=F32)
    act = (gate * _sigmoid(gate) * up).astype(BF16)
    acc_scr[...] += jnp.dot(act, wo_ref[...], preferred_element_type=F32)

    @pl.when(j == pl.num_programs(1) - 1)
    def _():
        o_ref[...] = x_ref[...] + gt_ref[0] * _rms(acc_scr[...], gpost_ref[...])


def _ffn(x, sc, sh, gt, g_pre, g_post, w_ffn_in, w_ffn_out, bm, tf):
    m, d = x.shape
    d_ff = w_ffn_out.shape[0]
    nf = d_ff // tf
    tiles_per_group = m // bm // sc.shape[0]
    r = sc.shape[1]
    mod_spec = pl.BlockSpec((1, r, d), lambda i, j: (i // tiles_per_group, 0, 0))
    vec_spec = pl.BlockSpec((1, d), lambda i, j: (0, 0))
    return pl.pallas_call(
        _ffn_kernel,
        grid=(m // bm, nf),
        in_specs=[
            pl.BlockSpec((bm, d), lambda i, j: (i, 0)),
            mod_spec, mod_spec, mod_spec, vec_spec, vec_spec,
            pl.BlockSpec((d, tf), lambda i, j: (0, j)),
            pl.BlockSpec((d, tf), lambda i, j: (0, nf + j)),
            pl.BlockSpec((tf, d), lambda i, j: (j, 0)),
        ],
        out_specs=pl.BlockSpec((bm, d), lambda i, j: (i, 0)),
        out_shape=jax.ShapeDtypeStruct((m, d), F32),
        scratch_shapes=[pltpu.VMEM((bm, d), BF16), pltpu.VMEM((bm, d), F32)],
        compiler_params=_params(("arbitrary", "arbitrary")),
    )(x, sc, sh, gt, g_pre.reshape(1, d), g_post.reshape(1, d), w_ffn_in, w_ffn_in, w_ffn_out)


def kernel(x_prompt, x_sample, c_prompt, c_sample, cache_k, cache_v, page_table, w_ada, b_ada, g_pre_mix, g_post_mix, w_in, ln_v_g, ln_v_b, w_s, b_s, sb_bias, w_branch_a, w_branch_b, w_out, g_pre_ffn, g_post_ffn, w_ffn_in, w_ffn_out):
    batch, seq, d = x_prompt.shape
    db, dseq, _ = x_sample.shape
    depth = w_ada.shape[0]
    assert dseq == 1, "the sample group decodes one token per sequence"
    mp, ms = batch * seq, db * dseq
    yp = x_prompt.reshape(mp, d)
    ys = x_sample.reshape(ms, d)
    c_all = jnp.concatenate([c_prompt, c_sample], axis=0)
    outs = {name: [] for name in ("kp", "vp", "ks", "vs", "cv")}

    for l in range(depth):
        w_in_l = w_in[l].astype(BF16)
        wa_l = w_branch_a[l].astype(BF16)
        wb_l = w_branch_b[l].astype(BF16)
        wo_l = w_out[l].astype(BF16)
        wfi_l = w_ffn_in[l].astype(BF16)
        wfo_l = w_ffn_out[l].astype(BF16)

        mods = _ada(c_all, w_ada[l], b_ada[l])
        mods_p = [m.reshape(batch, 1, d) for m in jnp.split(mods[:batch], 6, axis=-1)]
        mods_s = [m.reshape(1, db, d) for m in jnp.split(mods[batch:], 6, axis=-1)]

        sh1, sc1, gt1, sh2, sc2, gt2 = mods_p
        proj, kp, vp, _ = _inproj(yp, sc1, sh1, g_pre_mix[l], w_in_l, ln_v_g[l], ln_v_b[l], bm=512)
        b_out = _attn_prompt(proj, sb_bias[l], batch, seq, blk=256)
        x1 = _mix(proj, b_out, yp, gt1, g_post_mix[l], w_s[l], b_s[l].reshape(A_GROUPS, CHUNK, 1),
                  wa_l, wb_l, wo_l, bm=256, chunked=True)
        yp = _ffn(x1, sc2, sh2, gt2, g_pre_ffn[l], g_post_ffn[l], wfi_l, wfo_l, bm=512, tf=512)
        outs["kp"].append(kp.reshape(batch, seq, N_HEADS, HEAD_DIM))
        outs["vp"].append(vp.reshape(batch, seq, N_HEADS, HEAD_DIM))

        sh1, sc1, gt1, sh2, sc2, gt2 = mods_s
        proj, k_s, v_s, cv_s = _inproj(ys, sc1, sh1, g_pre_mix[l], w_in_l, ln_v_g[l], ln_v_b[l], bm=ms)
        q_s = proj[:, 2 * A_WIDTH:2 * A_WIDTH + SB_WIDTH].reshape(ms, N_HEADS, HEAD_DIM)
        b_out = _attn_sample(q_s, sb_bias[l], cache_k, cache_v, l, page_table, pages_per_step=4)
        ws_row = jnp.repeat(w_s[l][:, 0, 0], CHUNK).reshape(1, A_WIDTH)
        bs_row = jnp.repeat(b_s[l][:, 0], CHUNK).reshape(1, A_WIDTH)
        x1 = _mix(proj, b_out.reshape(ms, SB_WIDTH), ys, gt1, g_post_mix[l], ws_row, bs_row,
                  wa_l, wb_l, wo_l, bm=ms, chunked=False)
        ys = _ffn(x1, sc2, sh2, gt2, g_pre_ffn[l], g_post_ffn[l], wfi_l, wfo_l, bm=ms, tf=512)
        outs["ks"].append(k_s.reshape(db, dseq, N_HEADS, HEAD_DIM))
        outs["vs"].append(v_s.reshape(db, dseq, N_HEADS, HEAD_DIM))
        outs["cv"].append(cv_s.reshape(db, dseq, A_WIDTH))

    return (yp.reshape(batch, seq, d), ys.reshape(db, dseq, d),
            jnp.stack(outs["kp"]), jnp.stack(outs["vp"]),
            jnp.stack(outs["ks"]), jnp.stack(outs["vs"]), jnp.stack(outs["cv"]))
```

```python
import functools
import math

import jax
import jax.numpy as jnp
from jax import lax
from jax.experimental import pallas as pl
from jax.experimental.pallas import tpu as pltpu

F32 = jnp.float32
BF16 = jnp.bfloat16

EPS = 1e-6
LANE = 128
HEAD_DIM = 128
N_HEADS = 8
A_GROUPS = 8
CHUNK = 128
A_WIDTH = A_GROUPS * CHUNK
SB_WIDTH = N_HEADS * HEAD_DIM
COL_TILE = 1024
MXU_COLS = 256
ROW_SPLIT = 256
VMEM_LIMIT = 56 * 1024 * 1024


def _params(semantics):
    return pltpu.CompilerParams(dimension_semantics=semantics, vmem_limit_bytes=VMEM_LIMIT)


def _rms(x, g):
    return x * lax.rsqrt(jnp.mean(x * x, axis=-1, keepdims=True) + EPS) * g


def _sigmoid(x):
    return 1.0 / (1.0 + jnp.exp(-x))


def _ada_kernel(c_ref, w_ref, b_ref, o_ref):
    c = c_ref[...]
    a = (c * _sigmoid(c)).astype(BF16)
    o_ref[...] = jnp.dot(a, w_ref[...].astype(BF16), preferred_element_type=F32) + b_ref[...]


def _ada(c_all, w_ada, b_ada):
    rows, d = c_all.shape
    n = w_ada.shape[1]
    tn = 1024
    return pl.pallas_call(
        _ada_kernel,
        grid=(n // tn,),
        in_specs=[
            pl.BlockSpec((rows, d), lambda j: (0, 0)),
            pl.BlockSpec((d, tn), lambda j: (0, j)),
            pl.BlockSpec((1, tn), lambda j: (0, j)),
        ],
        out_specs=pl.BlockSpec((rows, tn), lambda j: (0, j)),
        out_shape=jax.ShapeDtypeStruct((rows, n), F32),
        compiler_params=_params(("arbitrary",)),
    )(c_all, w_ada, b_ada.reshape(1, n))


def _inproj_kernel(x_ref, sc_ref, sh_ref, g_ref, w_ref, lng_ref, lnb_ref,
                   proj_ref, k_ref, v_ref, *rest):
    va_ref, h_scr = rest if len(rest) == 2 else (None, rest[0])
    j = pl.program_id(1)
    bm = x_ref.shape[0]
    col_chunks = [slice(c, c + MXU_COLS) for c in range(0, COL_TILE, MXU_COLS)]

    def chunk(h, cs):
        return jnp.dot(h, w_ref[:, cs], preferred_element_type=F32)

    @pl.when(j == 0)
    def _():
        for r in range(0, bm, min(bm, ROW_SPLIT)):
            rs = slice(r, r + min(bm, ROW_SPLIT))
            sc = sc_ref[0] if sc_ref.shape[1] == 1 else sc_ref[0, rs]
            sh = sh_ref[0] if sh_ref.shape[1] == 1 else sh_ref[0, rs]
            h = (_rms(x_ref[rs, :], g_ref[...]) * (1.0 + sc) + sh).astype(BF16)
            h_scr[rs, :] = h
            for cs in col_chunks:
                proj_ref[rs, cs] = jax.nn.gelu(chunk(h, cs)).astype(BF16)

    @pl.when(j == 1)
    def _():
        h = h_scr[...]
        gs = [jax.nn.gelu(chunk(h, cs)) for cs in col_chunks]
        mean = sum(jnp.sum(g, axis=-1, keepdims=True) for g in gs) * (1.0 / COL_TILE)
        xcs = [g - mean for g in gs]
        var = sum(jnp.sum(xc * xc, axis=-1, keepdims=True) for xc in xcs) * (1.0 / COL_TILE)
        inv = lax.rsqrt(var + EPS)
        for cs, xc in zip(col_chunks, xcs):
            y = xc * inv * lng_ref[:, cs] + lnb_ref[:, cs]
            if va_ref is not None:
                va_ref[:, cs] = y
            proj_ref[:, cs] = y.astype(BF16)

    def raw(f32_ref):
        h = h_scr[...]
        for cs in col_chunks:
            acc = chunk(h, cs)
            if f32_ref is not None:
                f32_ref[:, cs] = acc
            proj_ref[:, cs] = acc.astype(BF16)

    pl.when(j == 2)(lambda: raw(None))
    pl.when(j == 3)(lambda: raw(k_ref))
    pl.when(j == 4)(lambda: raw(v_ref))

    @pl.when(j >= 5)
    def _():
        h = h_scr[...]
        for cs in col_chunks:
            proj_ref[:, cs] = _sigmoid(chunk(h, cs)).astype(BF16)


def _inproj(x, sc, sh, g_pre, w_in, ln_g, ln_b, bm, want_va):
    m, d = x.shape
    n_in = w_in.shape[1]
    tiles_per_group = m // bm // sc.shape[0]
    r = sc.shape[1]
    mod_spec = pl.BlockSpec((1, r, d), lambda i, j: (i // tiles_per_group, 0, 0))
    row_spec = pl.BlockSpec((bm, COL_TILE), lambda i, j: (i, 0))
    n_f32 = 3 if want_va else 2
    return pl.pallas_call(
        _inproj_kernel,
        grid=(m // bm, n_in // COL_TILE),
        in_specs=[
            pl.BlockSpec((bm, d), lambda i, j: (i, 0)),
            mod_spec, mod_spec,
            pl.BlockSpec((1, d), lambda i, j: (0, 0)),
            pl.BlockSpec((d, COL_TILE), lambda i, j: (0, j)),
            pl.BlockSpec((1, A_WIDTH), lambda i, j: (0, 0)),
            pl.BlockSpec((1, A_WIDTH), lambda i, j: (0, 0)),
        ],
        out_specs=[pl.BlockSpec((bm, COL_TILE), lambda i, j: (i, j))] + [row_spec] * n_f32,
        out_shape=[jax.ShapeDtypeStruct((m, n_in), BF16)]
        + [jax.ShapeDtypeStruct((m, COL_TILE), F32)] * n_f32,
        scratch_shapes=[pltpu.VMEM((bm, d), BF16)],
        compiler_params=_params(("arbitrary", "arbitrary")),
    )(x, sc, sh, g_pre.reshape(1, d), w_in, ln_g.reshape(1, A_WIDTH), ln_b.reshape(1, A_WIDTH))


def _log_terms(z):
    soft = jnp.log(1.0 + jnp.exp(-jnp.abs(z)))
    log_beta = jnp.minimum(z, 0.0) - soft
    return log_beta, log_beta - z


def _later_sum(x, tri):
    hi = x.astype(BF16)
    lo = (x - hi.astype(F32)).astype(BF16)
    return (jnp.dot(hi, tri, preferred_element_type=F32)
            + jnp.dot(lo, tri, preferred_element_type=F32))


def _strict_lower(n):
    row = lax.broadcasted_iota(jnp.int32, (n, n), 0)
    col = lax.broadcasted_iota(jnp.int32, (n, n), 1)
    return (row > col).astype(BF16)


MASKED_LOG = -1e30


def _attn_rows(q_ref, k_ref, v_ref, o_ref, bias, tri, causal, a, blk):
    n_blocks = a + 1
    n = n_blocks * blk
    q = q_ref[a * blk:(a + 1) * blk, :]
    s = lax.dot_general(q, k_ref[0:n, :], (((1,), (1,)), ((), ())), preferred_element_type=F32)
    yield
    z = s * (1.0 / math.sqrt(HEAD_DIM)) + bias
    log_beta, log_1m = _log_terms(z)
    lbs = [log_beta[:, j * blk:(j + 1) * blk] for j in range(n_blocks)]
    l1s = [log_1m[:, j * blk:(j + 1) * blk] for j in range(n_blocks)]
    lbs[-1] = jnp.where(causal, lbs[-1], MASKED_LOG)
    l1s[-1] = jnp.where(causal, l1s[-1], 0.0)
    stacked = jnp.concatenate(l1s, axis=0)
    hi = stacked.astype(BF16)
    lo = (stacked - hi.astype(F32)).astype(BF16)
    yield
    excl = (jnp.dot(hi, tri, preferred_element_type=F32)
            + jnp.dot(lo, tri, preferred_element_type=F32))
    yield
    excl = [excl[j * blk:(j + 1) * blk] for j in range(n_blocks)]
    run = jnp.zeros((blk, 1), F32)
    ws = [None] * n_blocks
    for j in reversed(range(n_blocks)):
        ws[j] = jnp.exp(lbs[j] + excl[j] + run).astype(BF16)
        run = run + excl[j][:, :1] + l1s[j][:, :1]
    w = jnp.concatenate(ws, axis=1)
    yield
    out = jnp.dot(w, v_ref[0:n, :], preferred_element_type=F32)
    o_ref[a * blk:(a + 1) * blk, :] = out.astype(BF16)


def _attn_kernel(bias_ref, q_ref, k_ref, v_ref, o_ref, *, blk, in_flight):
    bias = bias_ref[pl.program_id(1)]
    tri = _strict_lower(blk)
    row = lax.broadcasted_iota(jnp.int32, (blk, blk), 0)
    col = lax.broadcasted_iota(jnp.int32, (blk, blk), 1)
    causal = col < row
    waiting = [_attn_rows(q_ref, k_ref, v_ref, o_ref, bias, tri, causal, a, blk)
               for a in reversed(range(q_ref.shape[0] // blk))]
    active = []
    while waiting or active:
        active = [g for g in active if next(g, "done") != "done"]
        if waiting and len(active) < in_flight:
            g = waiting.pop(0)
            next(g)
            active.append(g)


def _attn_prompt(proj, sb_bias, batch, seq, blk, in_flight=3):
    q_col, k_col, v_col = (2 * A_WIDTH) // LANE, (2 * A_WIDTH + SB_WIDTH) // LANE, (2 * A_WIDTH + 2 * SB_WIDTH) // LANE

    def head_spec(col0):
        return pl.BlockSpec((seq, HEAD_DIM), lambda b, h: (b, col0 + h))

    return pl.pallas_call(
        functools.partial(_attn_kernel, blk=blk, in_flight=in_flight),
        grid=(batch, N_HEADS),
        in_specs=[pl.BlockSpec(memory_space=pltpu.SMEM), head_spec(q_col), head_spec(k_col), head_spec(v_col)],
        out_specs=head_spec(0),
        out_shape=jax.ShapeDtypeStruct((batch * seq, SB_WIDTH), BF16),
        compiler_params=_params(("arbitrary", "arbitrary")),
    )(sb_bias, proj, proj, proj)


def _attn_sample_kernel(pt_ref, q_ref, bias_ref, *refs, pages_per_step):
    k_refs = refs[:pages_per_step]
    v_refs = refs[pages_per_step:2 * pages_per_step]
    o_ref, run_scr, acc_scr = refs[2 * pages_per_step:]
    step = pl.program_id(1)
    page_cols = k_refs[0].shape[1]
    n_chunks = page_cols // LANE

    @pl.when(step == 0)
    def _():
        run_scr[...] = jnp.zeros_like(run_scr)
        acc_scr[...] = jnp.zeros_like(acc_scr)

    q = q_ref[0].astype(BF16)
    bias = bias_ref[...]
    tri_ones = jnp.concatenate([_strict_lower(LANE), jnp.ones((LANE, LANE), BF16)], axis=1)
    head = lax.broadcasted_iota(jnp.int32, (N_HEADS, page_cols), 0)
    col = lax.broadcasted_iota(jnp.int32, (N_HEADS, page_cols), 1)
    own = (col % N_HEADS) == head

    log_betas, sums = [], []
    for r in range(pages_per_step):
        k = k_refs[r][0].astype(BF16)
        s = lax.dot_general(q, k, (((1,), (1,)), ((), ())), preferred_element_type=F32)
        z = s * (1.0 / math.sqrt(HEAD_DIM)) + bias
        log_beta, log_1m = _log_terms(z)
        log_1m = jnp.where(own, log_1m, 0.0)
        chunks = [log_1m[:, c * LANE:(c + 1) * LANE] for c in range(n_chunks)]
        log_betas.append(log_beta)
        sums.append(_later_sum(jnp.concatenate(chunks, axis=0), tri_ones))

    run = run_scr[...]
    acc = acc_scr[...]
    for r in range(pages_per_step):
        later = [None] * n_chunks
        for c in reversed(range(n_chunks)):
            rows = sums[r][c * N_HEADS:(c + 1) * N_HEADS]
            later[c] = rows[:, :LANE] + run
            run = run + rows[:, LANE:]
        w = jnp.exp(log_betas[r] + jnp.concatenate(later, axis=1))
        w = jnp.where(own, w, 0.0).astype(BF16)
        acc = acc + jnp.dot(w, v_refs[r][0].astype(BF16), preferred_element_type=F32)
    run_scr[...] = run
    acc_scr[...] = acc

    @pl.when(step == pl.num_programs(1) - 1)
    def _():
        o_ref[0] = acc.astype(BF16)


def _attn_sample(q, sb_bias, cache_k, cache_v, layer, page_table, pages_per_step):
    db = q.shape[0]
    depth, n_pool, page, heads, d = cache_k.shape
    n_pages = page_table.shape[1]
    steps = n_pages // pages_per_step
    ck = cache_k.reshape(depth * n_pool, page * heads, d)
    cv = cache_v.reshape(depth * n_pool, page * heads, d)

    def page_spec(r):
        def index(b, s, pt):
            return (layer * n_pool + pt[b, n_pages - 1 - (s * pages_per_step + r)], 0, 0)
        return pl.BlockSpec((1, page * heads, d), index)

    grid_spec = pltpu.PrefetchScalarGridSpec(
        num_scalar_prefetch=1,
        grid=(db, steps),
        in_specs=[
            pl.BlockSpec((1, heads, d), lambda b, s, pt: (b, 0, 0)),
            pl.BlockSpec((heads, 1), lambda b, s, pt: (0, 0)),
        ] + [page_spec(r) for r in range(pages_per_step)] * 2,
        out_specs=pl.BlockSpec((1, heads, d), lambda b, s, pt: (b, 0, 0)),
        scratch_shapes=[pltpu.VMEM((heads, LANE), F32), pltpu.VMEM((heads, d), F32)],
    )
    return pl.pallas_call(
        functools.partial(_attn_sample_kernel, pages_per_step=pages_per_step),
        grid_spec=grid_spec,
        out_shape=jax.ShapeDtypeStruct((db, heads, d), BF16),
        compiler_params=_params(("arbitrary", "arbitrary")),
    )(page_table, q, sb_bias.reshape(heads, 1),
      *([ck] * pages_per_step), *([cv] * pages_per_step))


def _mix_kernel(u_ref, va_ref, b_ref, ga0_ref, ga1_ref, gb0_ref, gb1_ref, x_ref, gt_ref, gpost_ref,
                ws_ref, bs_ref, wa_ref, wb_ref, wo_ref, o_ref, *, chunked):
    bm = u_ref.shape[0]
    u = u_ref[...].astype(F32)
    if chunked:
        row = lax.broadcasted_iota(jnp.int32, (CHUNK, CHUNK), 0)
        col = lax.broadcasted_iota(jnp.int32, (CHUNK, CHUNK), 1)
        keep = col <= row
        cols = []
        for g in range(A_GROUPS):
            w = jnp.where(keep, ws_ref[g], 0.0).astype(BF16)
            b = bs_ref[g]
            rows = []
            for c in range(bm // CHUNK):
                v = va_ref[c * CHUNK:(c + 1) * CHUNK, g * CHUNK:(g + 1) * CHUNK]
                rows.append(jnp.dot(w, v, preferred_element_type=F32) + b)
            cols.append(jnp.concatenate(rows, axis=0))
        s = jnp.concatenate(cols, axis=1)
    else:
        s = va_ref[...].astype(F32) * ws_ref[...] + bs_ref[...]
    a = (u * s).astype(BF16)
    p = jnp.dot(a, wa_ref[...], preferred_element_type=F32)
    q = jnp.dot(b_ref[...], wb_ref[...], preferred_element_type=F32)
    half = p.shape[1] // 2
    merged = jnp.concatenate([
        ga0_ref[...].astype(F32) * p[:, :half] + gb0_ref[...].astype(F32) * q[:, :half],
        ga1_ref[...].astype(F32) * p[:, half:] + gb1_ref[...].astype(F32) * q[:, half:],
    ], axis=1).astype(BF16)
    mix = jnp.dot(merged, wo_ref[...], preferred_element_type=F32)
    o_ref[...] = x_ref[...] + gt_ref[0] * _rms(mix, gpost_ref[...])


def _mix(proj, b_out, x, gt, g_post, ws, bs, wa, wb, wo, bm, chunked):
    m, d = x.shape
    tiles_per_group = m // bm // gt.shape[0]
    r = gt.shape[1]

    def col_spec(c):
        return pl.BlockSpec((bm, COL_TILE), lambda i: (i, c))

    def whole(a):
        return pl.BlockSpec(a.shape, lambda i: (0,) * a.ndim)

    return pl.pallas_call(
        functools.partial(_mix_kernel, chunked=chunked),
        grid=(m // bm,),
        in_specs=[
            col_spec(0), col_spec(1),
            pl.BlockSpec((bm, SB_WIDTH), lambda i: (i, 0)),
            col_spec(5), col_spec(6), col_spec(7), col_spec(8),
            pl.BlockSpec((bm, d), lambda i: (i, 0)),
            pl.BlockSpec((1, r, d), lambda i: (i // tiles_per_group, 0, 0)),
            pl.BlockSpec((1, d), lambda i: (0, 0)),
            whole(ws), whole(bs), whole(wa), whole(wb), whole(wo),
        ],
        out_specs=pl.BlockSpec((bm, d), lambda i: (i, 0)),
        out_shape=jax.ShapeDtypeStruct((m, d), F32),
        compiler_params=_params(("arbitrary",)),
    )(proj, proj, b_out, proj, proj, proj, proj, x, gt, g_post.reshape(1, d), ws, bs, wa, wb, wo)


def _ffn_kernel(x_ref, sc_ref, sh_ref, gt_ref, gpre_ref, gpost_ref, wg_ref, wu_ref, wo_ref, o_ref,
                h_scr, acc_scr):
    j = pl.program_id(1)
    last = pl.num_programs(1) - 1
    bm = x_ref.shape[0]
    row_blocks = [slice(r, r + min(bm, ROW_SPLIT)) for r in range(0, bm, min(bm, ROW_SPLIT))]

    def rows_of(ref, rs):
        return ref[0] if ref.shape[1] == 1 else ref[0, rs]

    def partial_out(h):
        gate = jnp.dot(h, wg_ref[...], preferred_element_type=F32)
        up = jnp.dot(h, wu_ref[...], preferred_element_type=F32)
        act = (gate * _sigmoid(gate) * up).astype(BF16)
        return jnp.dot(act, wo_ref[...], preferred_element_type=F32)

    @pl.when(j == 0)
    def _():
        for rs in row_blocks:
            h = (_rms(x_ref[rs, :], gpre_ref[...]) * (1.0 + rows_of(sc_ref, rs))
                 + rows_of(sh_ref, rs)).astype(BF16)
            h_scr[rs, :] = h
            acc_scr[rs, :] = partial_out(h)

    @pl.when(jnp.logical_and(j > 0, j < last))
    def _():
        acc_scr[...] += partial_out(h_scr[...])

    @pl.when(j == last)
    def _():
        for rs in row_blocks:
            f = acc_scr[rs, :] + partial_out(h_scr[rs, :])
            o_ref[rs, :] = x_ref[rs, :] + rows_of(gt_ref, rs) * _rms(f, gpost_ref[...])


def _ffn(x, sc, sh, gt, g_pre, g_post, w_ffn_in, w_ffn_out, bm, tf):
    m, d = x.shape
    d_ff = w_ffn_out.shape[0]
    nf = d_ff // tf
    assert nf >= 2, "the kernel treats the first and last d_ff steps separately"
    tiles_per_group = m // bm // sc.shape[0]
    r = sc.shape[1]
    mod_spec = pl.BlockSpec((1, r, d), lambda i, j: (i // tiles_per_group, 0, 0))
    vec_spec = pl.BlockSpec((1, d), lambda i, j: (0, 0))
    return pl.pallas_call(
        _ffn_kernel,
        grid=(m // bm, nf),
        in_specs=[
            pl.BlockSpec((bm, d), lambda i, j: (i, 0)),
            mod_spec, mod_spec, mod_spec, vec_spec, vec_spec,
            pl.BlockSpec((d, tf), lambda i, j: (0, j)),
            pl.BlockSpec((d, tf), lambda i, j: (0, nf + j)),
            pl.BlockSpec((tf, d), lambda i, j: (j, 0)),
        ],
        out_specs=pl.BlockSpec((bm, d), lambda i, j: (i, 0)),
        out_shape=jax.ShapeDtypeStruct((m, d), F32),
        scratch_shapes=[pltpu.VMEM((bm, d), BF16), pltpu.VMEM((bm, d), F32)],
        compiler_params=_params(("arbitrary", "arbitrary")),
    )(x, sc, sh, gt, g_pre.reshape(1, d), g_post.reshape(1, d), w_ffn_in, w_ffn_in, w_ffn_out)


def kernel(x_prompt, x_sample, c_prompt, c_sample, cache_k, cache_v, page_table, w_ada, b_ada, g_pre_mix, g_post_mix, w_in, ln_v_g, ln_v_b, w_s, b_s, sb_bias, w_branch_a, w_branch_b, w_out, g_pre_ffn, g_post_ffn, w_ffn_in, w_ffn_out):
    batch, seq, d = x_prompt.shape
    db, dseq, _ = x_sample.shape
    depth = w_ada.shape[0]
    assert dseq == 1, "the sample group decodes one token per sequence"
    mp, ms = batch * seq, db * dseq
    yp = x_prompt.reshape(mp, d)
    ys = x_sample.reshape(ms, d)
    c_all = jnp.concatenate([c_prompt, c_sample], axis=0)
    outs = {name: [] for name in ("kp", "vp", "ks", "vs", "cv")}

    for l in range(depth):
        w_in_l = w_in[l].astype(BF16)
        wa_l = w_branch_a[l].astype(BF16)
        wb_l = w_branch_b[l].astype(BF16)
        wo_l = w_out[l].astype(BF16)
        wfi_l = w_ffn_in[l].astype(BF16)
        wfo_l = w_ffn_out[l].astype(BF16)

        mods = _ada(c_all, w_ada[l], b_ada[l])
        mods_p = [m.reshape(batch, 1, d) for m in jnp.split(mods[:batch], 6, axis=-1)]
        mods_s = [m.reshape(1, db, d) for m in jnp.split(mods[batch:], 6, axis=-1)]

        sh1, sc1, gt1, sh2, sc2, gt2 = mods_p
        proj, kp, vp = _inproj(yp, sc1, sh1, g_pre_mix[l], w_in_l, ln_v_g[l], ln_v_b[l],
                               bm=512, want_va=False)
        b_out = _attn_prompt(proj, sb_bias[l], batch, seq, blk=256)
        x1 = _mix(proj, b_out, yp, gt1, g_post_mix[l], w_s[l], b_s[l].reshape(A_GROUPS, CHUNK, 1),
                  wa_l, wb_l, wo_l, bm=256, chunked=True)
        yp = _ffn(x1, sc2, sh2, gt2, g_pre_ffn[l], g_post_ffn[l], wfi_l, wfo_l, bm=512, tf=512)
        outs["kp"].append(kp.reshape(batch, seq, N_HEADS, HEAD_DIM))
        outs["vp"].append(vp.reshape(batch, seq, N_HEADS, HEAD_DIM))

        sh1, sc1, gt1, sh2, sc2, gt2 = mods_s
        proj, k_s, v_s, cv_s = _inproj(ys, sc1, sh1, g_pre_mix[l], w_in_l, ln_v_g[l], ln_v_b[l],
                                       bm=ms, want_va=True)
        q_s = proj[:, 2 * A_WIDTH:2 * A_WIDTH + SB_WIDTH].reshape(ms, N_HEADS, HEAD_DIM)
        b_out = _attn_sample(q_s, sb_bias[l], cache_k, cache_v, l, page_table, pages_per_step=8)
        ws_row = jnp.repeat(w_s[l][:, 0, 0], CHUNK).reshape(1, A_WIDTH)
        bs_row = jnp.repeat(b_s[l][:, 0], CHUNK).reshape(1, A_WIDTH)
        x1 = _mix(proj, b_out.reshape(ms, SB_WIDTH), ys, gt1, g_post_mix[l], ws_row, bs_row,
                  wa_l, wb_l, wo_l, bm=ms, chunked=False)
        ys = _ffn(x1, sc2, sh2, gt2, g_pre_ffn[l], g_post_ffn[l], wfi_l, wfo_l, bm=ms, tf=512)
        outs["ks"].append(k_s.reshape(db, dseq, N_HEADS, HEAD_DIM))
        outs["vs"].append(v_s.reshape(db, dseq, N_HEADS, HEAD_DIM))
        outs["cv"].append(cv_s.reshape(db, dseq, A_WIDTH))

    return (yp.reshape(batch, seq, d), ys.reshape(db, dseq, d),
            jnp.stack(outs["kp"]), jnp.stack(outs["vp"]),
            jnp.stack(outs["ks"]), jnp.stack(outs["vs"]), jnp.stack(outs["cv"]))
```

```python
import functools
import math

import jax
import jax.numpy as jnp
from jax import lax
from jax.experimental import pallas as pl
from jax.experimental.pallas import tpu as pltpu

F32 = jnp.float32
BF16 = jnp.bfloat16

EPS = 1e-6
LANE = 128
HEAD_DIM = 128
N_HEADS = 8
A_GROUPS = 8
CHUNK = 128
A_WIDTH = A_GROUPS * CHUNK
SB_WIDTH = N_HEADS * HEAD_DIM
COL_TILE = 1024
MXU_COLS = 256
ROW_SPLIT = 256
VMEM_LIMIT = 56 * 1024 * 1024


def _params(semantics):
    return pltpu.CompilerParams(dimension_semantics=semantics, vmem_limit_bytes=VMEM_LIMIT)


def _rms(x, g):
    return x * lax.rsqrt(jnp.mean(x * x, axis=-1, keepdims=True) + EPS) * g


def _sigmoid(x):
    return 1.0 / (1.0 + jnp.exp(-x))


def _ada_kernel(c_ref, w_ref, b_ref, o_ref):
    c = c_ref[...]
    a = (c * _sigmoid(c)).astype(BF16)
    o_ref[...] = jnp.dot(a, w_ref[...].astype(BF16), preferred_element_type=F32) + b_ref[...]


def _ada(c_all, w_ada, b_ada):
    rows, d = c_all.shape
    n = w_ada.shape[1]
    tn = 1024
    return pl.pallas_call(
        _ada_kernel,
        grid=(n // tn,),
        in_specs=[
            pl.BlockSpec((rows, d), lambda j: (0, 0)),
            pl.BlockSpec((d, tn), lambda j: (0, j)),
            pl.BlockSpec((1, tn), lambda j: (0, j)),
        ],
        out_specs=pl.BlockSpec((rows, tn), lambda j: (0, j)),
        out_shape=jax.ShapeDtypeStruct((rows, n), F32),
        compiler_params=_params(("arbitrary",)),
    )(c_all, w_ada, b_ada.reshape(1, n))


def _inproj_kernel(x_ref, sc_ref, sh_ref, g_ref, w_ref, lng_ref, lnb_ref,
                   proj_ref, k_ref, v_ref, *rest):
    va_ref, h_scr = rest if len(rest) == 2 else (None, rest[0])
    j = pl.program_id(1)
    bm = x_ref.shape[0]
    col_chunks = [slice(c, c + MXU_COLS) for c in range(0, COL_TILE, MXU_COLS)]

    def chunk(h, cs):
        return jnp.dot(h, w_ref[:, cs], preferred_element_type=F32)

    @pl.when(j == 0)
    def _():
        for r in range(0, bm, min(bm, ROW_SPLIT)):
            rs = slice(r, r + min(bm, ROW_SPLIT))
            sc = sc_ref[0] if sc_ref.shape[1] == 1 else sc_ref[0, rs]
            sh = sh_ref[0] if sh_ref.shape[1] == 1 else sh_ref[0, rs]
            h = (_rms(x_ref[rs, :], g_ref[...]) * (1.0 + sc) + sh).astype(BF16)
            h_scr[rs, :] = h
            for cs in col_chunks:
                proj_ref[rs, cs] = jax.nn.gelu(chunk(h, cs)).astype(BF16)

    @pl.when(j == 1)
    def _():
        h = h_scr[...]
        gs = [jax.nn.gelu(chunk(h, cs)) for cs in col_chunks]
        mean = sum(jnp.sum(g, axis=-1, keepdims=True) for g in gs) * (1.0 / COL_TILE)
        xcs = [g - mean for g in gs]
        var = sum(jnp.sum(xc * xc, axis=-1, keepdims=True) for xc in xcs) * (1.0 / COL_TILE)
        inv = lax.rsqrt(var + EPS)
        for cs, xc in zip(col_chunks, xcs):
            y = xc * inv * lng_ref[:, cs] + lnb_ref[:, cs]
            if va_ref is not None:
                va_ref[:, cs] = y
            proj_ref[:, cs] = y.astype(BF16)

    def raw(f32_ref):
        h = h_scr[...]
        for cs in col_chunks:
            acc = chunk(h, cs)
            if f32_ref is not None:
                f32_ref[:, cs] = acc
            proj_ref[:, cs] = acc.astype(BF16)

    pl.when(j == 2)(lambda: raw(None))
    pl.when(j == 3)(lambda: raw(k_ref))
    pl.when(j == 4)(lambda: raw(v_ref))

    @pl.when(j >= 5)
    def _():
        h = h_scr[...]
        for cs in col_chunks:
            proj_ref[:, cs] = _sigmoid(chunk(h, cs)).astype(BF16)


def _inproj(x, sc, sh, g_pre, w_in, ln_g, ln_b, bm, want_va):
    m, d = x.shape
    n_in = w_in.shape[1]
    tiles_per_group = m // bm // sc.shape[0]
    r = sc.shape[1]
    mod_spec = pl.BlockSpec((1, r, d), lambda i, j: (i // tiles_per_group, 0, 0))
    row_spec = pl.BlockSpec((bm, COL_TILE), lambda i, j: (i, 0))
    n_f32 = 3 if want_va else 2
    return pl.pallas_call(
        _inproj_kernel,
        grid=(m // bm, n_in // COL_TILE),
        in_specs=[
            pl.BlockSpec((bm, d), lambda i, j: (i, 0)),
            mod_spec, mod_spec,
            pl.BlockSpec((1, d), lambda i, j: (0, 0)),
            pl.BlockSpec((d, COL_TILE), lambda i, j: (0, j)),
            pl.BlockSpec((1, A_WIDTH), lambda i, j: (0, 0)),
            pl.BlockSpec((1, A_WIDTH), lambda i, j: (0, 0)),
        ],
        out_specs=[pl.BlockSpec((bm, COL_TILE), lambda i, j: (i, j))] + [row_spec] * n_f32,
        out_shape=[jax.ShapeDtypeStruct((m, n_in), BF16)]
        + [jax.ShapeDtypeStruct((m, COL_TILE), F32)] * n_f32,
        scratch_shapes=[pltpu.VMEM((bm, d), BF16)],
        compiler_params=_params(("arbitrary", "arbitrary")),
    )(x, sc, sh, g_pre.reshape(1, d), w_in, ln_g.reshape(1, A_WIDTH), ln_b.reshape(1, A_WIDTH))


def _log_terms(z):
    soft = jnp.log(1.0 + jnp.exp(-jnp.abs(z)))
    log_beta = jnp.minimum(z, 0.0) - soft
    return log_beta, log_beta - z


def _strict_lower(n):
    row = lax.broadcasted_iota(jnp.int32, (n, n), 0)
    col = lax.broadcasted_iota(jnp.int32, (n, n), 1)
    return (row > col).astype(BF16)


MASKED_LOG = -1e30


def _attn_rows(q_ref, k_ref, v_ref, o_ref, bias, tri, causal, a, blk):
    n_blocks = a + 1
    n = n_blocks * blk
    q = q_ref[a * blk:(a + 1) * blk, :]
    s = lax.dot_general(q, k_ref[0:n, :], (((1,), (1,)), ((), ())), preferred_element_type=F32)
    yield
    z = s * (1.0 / math.sqrt(HEAD_DIM)) + bias
    log_beta, log_1m = _log_terms(z)
    lbs = [log_beta[:, j * blk:(j + 1) * blk] for j in range(n_blocks)]
    l1s = [log_1m[:, j * blk:(j + 1) * blk] for j in range(n_blocks)]
    lbs[-1] = jnp.where(causal, lbs[-1], MASKED_LOG)
    l1s[-1] = jnp.where(causal, l1s[-1], 0.0)
    stacked = jnp.concatenate(l1s, axis=0)
    hi = stacked.astype(BF16)
    lo = (stacked - hi.astype(F32)).astype(BF16)
    yield
    excl = (jnp.dot(hi, tri, preferred_element_type=F32)
            + jnp.dot(lo, tri, preferred_element_type=F32))
    yield
    excl = [excl[j * blk:(j + 1) * blk] for j in range(n_blocks)]
    run = jnp.zeros((blk, 1), F32)
    ws = [None] * n_blocks
    for j in reversed(range(n_blocks)):
        ws[j] = jnp.exp(lbs[j] + excl[j] + run).astype(BF16)
        run = run + excl[j][:, :1] + l1s[j][:, :1]
    w = jnp.concatenate(ws, axis=1)
    yield
    out = jnp.dot(w, v_ref[0:n, :], preferred_element_type=F32)
    o_ref[a * blk:(a + 1) * blk, :] = out.astype(BF16)


def _sample_pages(q, bias, k_refs, v_refs, run_scr, acc_scr):
    page_cols = k_refs[0].shape[1]
    n_chunks = page_cols // LANE
    head = lax.broadcasted_iota(jnp.int32, (N_HEADS, page_cols), 0)
    col = lax.broadcasted_iota(jnp.int32, (N_HEADS, page_cols), 1)
    own = (col % N_HEADS) == head
    tri_ones = jnp.concatenate([_strict_lower(LANE), jnp.ones((LANE, LANE), BF16)], axis=1)
    scores = [lax.dot_general(q, k[0].astype(BF16), (((1,), (1,)), ((), ())),
                              preferred_element_type=F32) for k in k_refs]
    yield
    log_betas, stacked = [], []
    for s in scores:
        log_beta, log_1m = _log_terms(s * (1.0 / math.sqrt(HEAD_DIM)) + bias)
        log_1m = jnp.where(own, log_1m, 0.0)
        log_betas.append(log_beta)
        stacked += [log_1m[:, c * LANE:(c + 1) * LANE] for c in range(n_chunks)]
    stacked = jnp.concatenate(stacked, axis=0)
    hi = stacked.astype(BF16)
    lo = (stacked - hi.astype(F32)).astype(BF16)
    yield
    sums = (jnp.dot(hi, tri_ones, preferred_element_type=F32)
            + jnp.dot(lo, tri_ones, preferred_element_type=F32))
    yield
    run = run_scr[...]
    weights = []
    for r in range(len(k_refs)):
        later = [None] * n_chunks
        for c in reversed(range(n_chunks)):
            rows = sums[(r * n_chunks + c) * N_HEADS:(r * n_chunks + c + 1) * N_HEADS]
            later[c] = rows[:, :LANE] + run
            run = run + rows[:, LANE:]
        w = jnp.exp(log_betas[r] + jnp.concatenate(later, axis=1))
        weights.append(jnp.where(own, w, 0.0).astype(BF16))
    run_scr[...] = run
    yield
    acc = acc_scr[...]
    for w, v in zip(weights, v_refs):
        acc = acc + jnp.dot(w, v[0].astype(BF16), preferred_element_type=F32)
    acc_scr[...] = acc


def _interleave(generators, in_flight):
    waiting, active = list(generators), []
    while waiting or active:
        active = [g for g in active if next(g, "done") != "done"]
        if waiting and len(active) < in_flight:
            g = waiting.pop(0)
            next(g)
            active.append(g)


def _attn_kernel(pt_ref, bias_ref, q_ref, k_ref, v_ref, qs_ref, bias_col_ref, *refs,
                 blk, groups, pages_per_step, steps_per_seq, in_flight):
    del pt_ref
    k_pages = refs[:pages_per_step]
    v_pages = refs[pages_per_step:2 * pages_per_step]
    o_ref, os_ref, run_scr, acc_scr = refs[2 * pages_per_step:]
    group = pl.program_id(2)
    step = (pl.program_id(0) * pl.num_programs(1) + pl.program_id(1)) * groups + group
    within = lax.rem(step, steps_per_seq)
    n_blocks = q_ref.shape[0] // blk

    @pl.when(within == 0)
    def _():
        run_scr[...] = jnp.zeros_like(run_scr)
        acc_scr[...] = jnp.zeros_like(acc_scr)

    bias = bias_ref[pl.program_id(1)]
    tri = _strict_lower(blk)
    row = lax.broadcasted_iota(jnp.int32, (blk, blk), 0)
    col = lax.broadcasted_iota(jnp.int32, (blk, blk), 1)
    causal = col < row

    def body(p):
        mine = [a for a in reversed(range(n_blocks)) if a % (2 * groups) in (p, 2 * groups - 1 - p)]
        _interleave(
            [_sample_pages(qs_ref[0], bias_col_ref[...], k_pages, v_pages, run_scr, acc_scr)]
            + [_attn_rows(q_ref, k_ref, v_ref, o_ref, bias, tri, causal, a, blk) for a in mine],
            in_flight)

    for p in range(groups):
        pl.when(group == p)(functools.partial(body, p))

    @pl.when(within == steps_per_seq - 1)
    def _():
        os_ref[0] = acc_scr[...].astype(BF16)


def _attn(proj, sb_bias, batch, seq, q_s, cache_k, cache_v, layer, page_table, blk, groups, in_flight):
    db = q_s.shape[0]
    depth, n_pool, page, heads, d = cache_k.shape
    n_pages = page_table.shape[1]
    steps = batch * heads * groups
    assert (seq // blk) % (2 * groups) == 0 and (db * n_pages) % steps == 0
    pages_per_step = db * n_pages // steps
    assert n_pages % pages_per_step == 0
    steps_per_seq = n_pages // pages_per_step
    ck = cache_k.reshape(depth * n_pool, page * heads, d)
    cv = cache_v.reshape(depth * n_pool, page * heads, d)
    q_col, k_col, v_col = (2 * A_WIDTH) // LANE, (2 * A_WIDTH + SB_WIDTH) // LANE, (2 * A_WIDTH + 2 * SB_WIDTH) // LANE

    def head_spec(col0):
        return pl.BlockSpec((seq, HEAD_DIM), lambda b, h, p, pt, bias: (b, col0 + h))

    def seq_and_part(b, h, p):
        step = (b * heads + h) * groups + p
        return lax.div(step, steps_per_seq), lax.rem(step, steps_per_seq)

    def seq_spec():
        return pl.BlockSpec((1, heads, d), lambda b, h, p, pt, bias: (seq_and_part(b, h, p)[0], 0, 0))

    def page_spec(r):
        def index(b, h, p, pt, bias):
            s, part = seq_and_part(b, h, p)
            newest_first = n_pages - 1 - (part * pages_per_step + r)
            return (layer * n_pool + pt[s, newest_first], 0, 0)
        return pl.BlockSpec((1, page * heads, d), index)

    grid_spec = pltpu.PrefetchScalarGridSpec(
        num_scalar_prefetch=2,
        grid=(batch, heads, groups),
        in_specs=[head_spec(q_col), head_spec(k_col), head_spec(v_col), seq_spec(),
                  pl.BlockSpec((heads, 1), lambda b, h, p, pt, bias: (0, 0))]
        + [page_spec(r) for r in range(pages_per_step)] * 2,
        out_specs=[head_spec(0), seq_spec()],
        scratch_shapes=[pltpu.VMEM((heads, LANE), F32), pltpu.VMEM((heads, d), F32)],
    )
    return pl.pallas_call(
        functools.partial(_attn_kernel, blk=blk, groups=groups, pages_per_step=pages_per_step,
                          steps_per_seq=steps_per_seq, in_flight=in_flight),
        grid_spec=grid_spec,
        out_shape=[jax.ShapeDtypeStruct((batch * seq, SB_WIDTH), BF16),
                   jax.ShapeDtypeStruct((db, heads, d), BF16)],
        compiler_params=_params(("arbitrary", "arbitrary", "arbitrary")),
    )(page_table, sb_bias, proj, proj, proj, q_s, sb_bias.reshape(heads, 1),
      *([ck] * pages_per_step), *([cv] * pages_per_step))


def _mix_kernel(u_ref, va_ref, b_ref, ga0_ref, ga1_ref, gb0_ref, gb1_ref, x_ref, gt_ref, gpost_ref,
                ws_ref, bs_ref, wa_ref, wb_ref, wo_ref, o_ref, *, chunked):
    bm = u_ref.shape[0]
    u = u_ref[...].astype(F32)
    if chunked:
        row = lax.broadcasted_iota(jnp.int32, (CHUNK, CHUNK), 0)
        col = lax.broadcasted_iota(jnp.int32, (CHUNK, CHUNK), 1)
        keep = col <= row
        cols = []
        for g in range(A_GROUPS):
            w = jnp.where(keep, ws_ref[g], 0.0).astype(BF16)
            b = bs_ref[g]
            rows = []
            for c in range(bm // CHUNK):
                v = va_ref[c * CHUNK:(c + 1) * CHUNK, g * CHUNK:(g + 1) * CHUNK]
                rows.append(jnp.dot(w, v, preferred_element_type=F32) + b)
            cols.append(jnp.concatenate(rows, axis=0))
        s = jnp.concatenate(cols, axis=1)
    else:
        s = va_ref[...].astype(F32) * ws_ref[...] + bs_ref[...]
    a = (u * s).astype(BF16)
    p = jnp.dot(a, wa_ref[...], preferred_element_type=F32)
    q = jnp.dot(b_ref[...], wb_ref[...], preferred_element_type=F32)
    half = p.shape[1] // 2
    merged = jnp.concatenate([
        ga0_ref[...].astype(F32) * p[:, :half] + gb0_ref[...].astype(F32) * q[:, :half],
        ga1_ref[...].astype(F32) * p[:, half:] + gb1_ref[...].astype(F32) * q[:, half:],
    ], axis=1).astype(BF16)
    mix = jnp.dot(merged, wo_ref[...], preferred_element_type=F32)
    o_ref[...] = x_ref[...] + gt_ref[0] * _rms(mix, gpost_ref[...])


def _mix(proj, b_out, x, gt, g_post, ws, bs, wa, wb, wo, bm, chunked):
    m, d = x.shape
    tiles_per_group = m // bm // gt.shape[0]
    r = gt.shape[1]

    def col_spec(c):
        return pl.BlockSpec((bm, COL_TILE), lambda i: (i, c))

    def whole(a):
        return pl.BlockSpec(a.shape, lambda i: (0,) * a.ndim)

    return pl.pallas_call(
        functools.partial(_mix_kernel, chunked=chunked),
        grid=(m // bm,),
        in_specs=[
            col_spec(0), col_spec(1),
            pl.BlockSpec((bm, SB_WIDTH), lambda i: (i, 0)),
            col_spec(5), col_spec(6), col_spec(7), col_spec(8),
            pl.BlockSpec((bm, d), lambda i: (i, 0)),
            pl.BlockSpec((1, r, d), lambda i: (i // tiles_per_group, 0, 0)),
            pl.BlockSpec((1, d), lambda i: (0, 0)),
            whole(ws), whole(bs), whole(wa), whole(wb), whole(wo),
        ],
        out_specs=pl.BlockSpec((bm, d), lambda i: (i, 0)),
        out_shape=jax.ShapeDtypeStruct((m, d), F32),
        compiler_params=_params(("arbitrary",)),
    )(proj, proj, b_out, proj, proj, proj, proj, x, gt, g_post.reshape(1, d), ws, bs, wa, wb, wo)


def _ffn_kernel(x_ref, sc_ref, sh_ref, gt_ref, gpre_ref, gpost_ref, wg_ref, wu_ref, wo_ref, o_ref,
                h_scr, acc_scr):
    j = pl.program_id(1)
    last = pl.num_programs(1) - 1
    bm = x_ref.shape[0]
    row_blocks = [slice(r, r + min(bm, ROW_SPLIT)) for r in range(0, bm, min(bm, ROW_SPLIT))]

    def rows_of(ref, rs):
        return ref[0] if ref.shape[1] == 1 else ref[0, rs]

    def partial_out(h):
        gate = jnp.dot(h, wg_ref[...], preferred_element_type=F32)
        up = jnp.dot(h, wu_ref[...], preferred_element_type=F32)
        act = (gate * _sigmoid(gate) * up).astype(BF16)
        return jnp.dot(act, wo_ref[...], preferred_element_type=F32)

    @pl.when(j == 0)
    def _():
        for rs in row_blocks:
            h = (_rms(x_ref[rs, :], gpre_ref[...]) * (1.0 + rows_of(sc_ref, rs))
                 + rows_of(sh_ref, rs)).astype(BF16)
            h_scr[rs, :] = h
            acc_scr[rs, :] = partial_out(h)

    @pl.when(jnp.logical_and(j > 0, j < last))
    def _():
        acc_scr[...] += partial_out(h_scr[...])

    @pl.when(j == last)
    def _():
        for rs in row_blocks:
            f = acc_scr[rs, :] + partial_out(h_scr[rs, :])
            o_ref[rs, :] = x_ref[rs, :] + rows_of(gt_ref, rs) * _rms(f, gpost_ref[...])


def _ffn(x, sc, sh, gt, g_pre, g_post, w_ffn_in, w_ffn_out, bm, tf):
    m, d = x.shape
    d_ff = w_ffn_out.shape[0]
    nf = d_ff // tf
    assert nf >= 2, "the kernel treats the first and last d_ff steps separately"
    tiles_per_group = m // bm // sc.shape[0]
    r = sc.shape[1]
    mod_spec = pl.BlockSpec((1, r, d), lambda i, j: (i // tiles_per_group, 0, 0))
    vec_spec = pl.BlockSpec((1, d), lambda i, j: (0, 0))
    return pl.pallas_call(
        _ffn_kernel,
        grid=(m // bm, nf),
        in_specs=[
            pl.BlockSpec((bm, d), lambda i, j: (i, 0)),
            mod_spec, mod_spec, mod_spec, vec_spec, vec_spec,
            pl.BlockSpec((d, tf), lambda i, j: (0, j)),
            pl.BlockSpec((d, tf), lambda i, j: (0, nf + j)),
            pl.BlockSpec((tf, d), lambda i, j: (j, 0)),
        ],
        out_specs=pl.BlockSpec((bm, d), lambda i, j: (i, 0)),
        out_shape=jax.ShapeDtypeStruct((m, d), F32),
        scratch_shapes=[pltpu.VMEM((bm, d), BF16), pltpu.VMEM((bm, d), F32)],
        compiler_params=_params(("arbitrary", "arbitrary")),
    )(x, sc, sh, gt, g_pre.reshape(1, d), g_post.reshape(1, d), w_ffn_in, w_ffn_in, w_ffn_out)


def kernel(x_prompt, x_sample, c_prompt, c_sample, cache_k, cache_v, page_table, w_ada, b_ada, g_pre_mix, g_post_mix, w_in, ln_v_g, ln_v_b, w_s, b_s, sb_bias, w_branch_a, w_branch_b, w_out, g_pre_ffn, g_post_ffn, w_ffn_in, w_ffn_out):
    batch, seq, d = x_prompt.shape
    db, dseq, _ = x_sample.shape
    depth = w_ada.shape[0]
    assert dseq == 1, "the sample group decodes one token per sequence"
    mp, ms = batch * seq, db * dseq
    yp = x_prompt.reshape(mp, d)
    ys = x_sample.reshape(ms, d)
    c_all = jnp.concatenate([c_prompt, c_sample], axis=0)
    outs = {name: [] for name in ("kp", "vp", "ks", "vs", "cv")}

    for l in range(depth):
        w_in_l = w_in[l].astype(BF16)
        wa_l = w_branch_a[l].astype(BF16)
        wb_l = w_branch_b[l].astype(BF16)
        wo_l = w_out[l].astype(BF16)
        wfi_l = w_ffn_in[l].astype(BF16)
        wfo_l = w_ffn_out[l].astype(BF16)

        mods = _ada(c_all, w_ada[l], b_ada[l])
        mods_p = [m.reshape(batch, 1, d) for m in jnp.split(mods[:batch], 6, axis=-1)]
        mods_s = [m.reshape(1, db, d) for m in jnp.split(mods[batch:], 6, axis=-1)]

        sh1, sc1, gt1, sh2, sc2, gt2 = mods_p
        sh1_s, sc1_s, gt1_s, sh2_s, sc2_s, gt2_s = mods_s
        proj, kp, vp = _inproj(yp, sc1, sh1, g_pre_mix[l], w_in_l, ln_v_g[l], ln_v_b[l],
                               bm=512, want_va=False)
        proj_s, k_s, v_s, cv_s = _inproj(ys, sc1_s, sh1_s, g_pre_mix[l], w_in_l, ln_v_g[l], ln_v_b[l],
                                         bm=ms, want_va=True)
        q_s = proj_s[:, 2 * A_WIDTH:2 * A_WIDTH + SB_WIDTH].reshape(ms, N_HEADS, HEAD_DIM)
        b_out, b_out_s = _attn(proj, sb_bias[l], batch, seq, q_s, cache_k, cache_v, l, page_table,
                               blk=256, groups=2, in_flight=4)

        x1 = _mix(proj, b_out, yp, gt1, g_post_mix[l], w_s[l], b_s[l].reshape(A_GROUPS, CHUNK, 1),
                  wa_l, wb_l, wo_l, bm=256, chunked=True)
        yp = _ffn(x1, sc2, sh2, gt2, g_pre_ffn[l], g_post_ffn[l], wfi_l, wfo_l, bm=512, tf=512)
        outs["kp"].append(kp.reshape(batch, seq, N_HEADS, HEAD_DIM))
        outs["vp"].append(vp.reshape(batch, seq, N_HEADS, HEAD_DIM))

        ws_row = jnp.repeat(w_s[l][:, 0, 0], CHUNK).reshape(1, A_WIDTH)
        bs_row = jnp.repeat(b_s[l][:, 0], CHUNK).reshape(1, A_WIDTH)
        x1 = _mix(proj_s, b_out_s.reshape(ms, SB_WIDTH), ys, gt1_s, g_post_mix[l], ws_row, bs_row,
                  wa_l, wb_l, wo_l, bm=ms, chunked=False)
        ys = _ffn(x1, sc2_s, sh2_s, gt2_s, g_pre_ffn[l], g_post_ffn[l], wfi_l, wfo_l, bm=ms, tf=512)
        outs["ks"].append(k_s.reshape(db, dseq, N_HEADS, HEAD_DIM))
        outs["vs"].append(v_s.reshape(db, dseq, N_HEADS, HEAD_DIM))
        outs["cv"].append(cv_s.reshape(db, dseq, A_WIDTH))

    return (yp.reshape(batch, seq, d), ys.reshape(db, dseq, d),
            jnp.stack(outs["kp"]), jnp.stack(outs["vp"]),
            jnp.stack(outs["ks"]), jnp.stack(outs["vs"]), jnp.stack(outs["cv"]))
```

```python
import functools
import math

import jax
import jax.numpy as jnp
from jax import lax
from jax.experimental import pallas as pl
from jax.experimental.pallas import tpu as pltpu

F32 = jnp.float32
BF16 = jnp.bfloat16

EPS = 1e-6
LANE = 128
HEAD_DIM = 128
N_HEADS = 8
A_GROUPS = 8
CHUNK = 128
A_WIDTH = A_GROUPS * CHUNK
SB_WIDTH = N_HEADS * HEAD_DIM
COL_TILE = 1024
MXU_COLS = 256
ROW_SPLIT = 256
VMEM_LIMIT = 60 * 1024 * 1024


def _params(semantics):
    return pltpu.CompilerParams(dimension_semantics=semantics, vmem_limit_bytes=VMEM_LIMIT)


def _rms(x, g):
    return x * lax.rsqrt(jnp.mean(x * x, axis=-1, keepdims=True) + EPS) * g


def _sigmoid(x):
    return 1.0 / (1.0 + jnp.exp(-x))


def _ada_kernel(c_ref, w_ref, b_ref, o_ref):
    c = c_ref[...]
    a = (c * _sigmoid(c)).astype(BF16)
    o_ref[...] = jnp.dot(a, w_ref[...].astype(BF16), preferred_element_type=F32) + b_ref[...]


def _ada(c_all, w_ada, b_ada):
    rows, d = c_all.shape
    n = w_ada.shape[1]
    tn = 1024
    return pl.pallas_call(
        _ada_kernel,
        grid=(n // tn,),
        in_specs=[
            pl.BlockSpec((rows, d), lambda j: (0, 0)),
            pl.BlockSpec((d, tn), lambda j: (0, j)),
            pl.BlockSpec((1, tn), lambda j: (0, j)),
        ],
        out_specs=pl.BlockSpec((rows, tn), lambda j: (0, j)),
        out_shape=jax.ShapeDtypeStruct((rows, n), F32),
        compiler_params=_params(("arbitrary",)),
    )(c_all, w_ada, b_ada.reshape(1, n))


def _inproj_kernel(x_ref, sc_ref, sh_ref, g_ref, w_ref, lng_ref, lnb_ref,
                   proj_ref, k_ref, v_ref, *rest):
    va_ref, h_scr = rest if len(rest) == 2 else (None, rest[0])
    j = pl.program_id(1)
    bm = x_ref.shape[0]
    col_chunks = [slice(c, c + MXU_COLS) for c in range(0, COL_TILE, MXU_COLS)]

    def chunk(h, cs):
        return jnp.dot(h, w_ref[:, cs], preferred_element_type=F32)

    @pl.when(j == 0)
    def _():
        for r in range(0, bm, min(bm, ROW_SPLIT)):
            rs = slice(r, r + min(bm, ROW_SPLIT))
            sc = sc_ref[0] if sc_ref.shape[1] == 1 else sc_ref[0, rs]
            sh = sh_ref[0] if sh_ref.shape[1] == 1 else sh_ref[0, rs]
            h = (_rms(x_ref[rs, :], g_ref[...]) * (1.0 + sc) + sh).astype(BF16)
            h_scr[rs, :] = h
            for cs in col_chunks:
                proj_ref[rs, cs] = jax.nn.gelu(chunk(h, cs)).astype(BF16)

    @pl.when(j == 1)
    def _():
        for r in range(0, bm, min(bm, ROW_SPLIT)):
            rs = slice(r, r + min(bm, ROW_SPLIT))
            h = h_scr[rs, :]
            gs = [jax.nn.gelu(chunk(h, cs)) for cs in col_chunks]
            mean = sum(jnp.sum(g, axis=-1, keepdims=True) for g in gs) * (1.0 / COL_TILE)
            xcs = [g - mean for g in gs]
            var = sum(jnp.sum(xc * xc, axis=-1, keepdims=True) for xc in xcs) * (1.0 / COL_TILE)
            inv = lax.rsqrt(var + EPS)
            for cs, xc in zip(col_chunks, xcs):
                y = xc * inv * lng_ref[:, cs] + lnb_ref[:, cs]
                if va_ref is not None:
                    va_ref[rs, cs] = y
                proj_ref[rs, cs] = y.astype(BF16)

    def raw(f32_ref):
        h = h_scr[...]
        for cs in col_chunks:
            acc = chunk(h, cs)
            if f32_ref is not None:
                f32_ref[:, cs] = acc
            proj_ref[:, cs] = acc.astype(BF16)

    pl.when(j == 2)(lambda: raw(None))
    pl.when(j == 3)(lambda: raw(k_ref))
    pl.when(j == 4)(lambda: raw(v_ref))

    @pl.when(j >= 5)
    def _():
        h = h_scr[...]
        for cs in col_chunks:
            proj_ref[:, cs] = _sigmoid(chunk(h, cs)).astype(BF16)


def _inproj(x, sc, sh, g_pre, w_in, ln_g, ln_b, bm, want_va):
    m, d = x.shape
    n_in = w_in.shape[1]
    tiles_per_group = m // bm // sc.shape[0]
    r = sc.shape[1]
    mod_spec = pl.BlockSpec((1, r, d), lambda i, j: (i // tiles_per_group, 0, 0))
    row_spec = pl.BlockSpec((bm, COL_TILE), lambda i, j: (i, 0))
    n_f32 = 3 if want_va else 2
    return pl.pallas_call(
        _inproj_kernel,
        grid=(m // bm, n_in // COL_TILE),
        in_specs=[
            pl.BlockSpec((bm, d), lambda i, j: (i, 0)),
            mod_spec, mod_spec,
            pl.BlockSpec((1, d), lambda i, j: (0, 0)),
            pl.BlockSpec((d, COL_TILE), lambda i, j: (0, j)),
            pl.BlockSpec((1, A_WIDTH), lambda i, j: (0, 0)),
            pl.BlockSpec((1, A_WIDTH), lambda i, j: (0, 0)),
        ],
        out_specs=[pl.BlockSpec((bm, COL_TILE), lambda i, j: (i, j))] + [row_spec] * n_f32,
        out_shape=[jax.ShapeDtypeStruct((m, n_in), BF16)]
        + [jax.ShapeDtypeStruct((m, COL_TILE), F32)] * n_f32,
        scratch_shapes=[pltpu.VMEM((bm, d), BF16)],
        compiler_params=_params(("arbitrary", "arbitrary")),
    )(x, sc, sh, g_pre.reshape(1, d), w_in, ln_g.reshape(1, A_WIDTH), ln_b.reshape(1, A_WIDTH))


def _log_terms(z):
    soft = jnp.log(1.0 + jnp.exp(-jnp.abs(z)))
    log_beta = jnp.minimum(z, 0.0) - soft
    return log_beta, log_beta - z


def _strict_lower(n):
    row = lax.broadcasted_iota(jnp.int32, (n, n), 0)
    col = lax.broadcasted_iota(jnp.int32, (n, n), 1)
    return (row > col).astype(BF16)


MASKED_LOG = -1e30


def _attn_rows(q_ref, k_ref, v_ref, o_ref, bias, tri, causal, a, blk):
    n_blocks = a + 1
    n = n_blocks * blk
    q = q_ref[a * blk:(a + 1) * blk, :]
    s = lax.dot_general(q, k_ref[0:n, :], (((1,), (1,)), ((), ())), preferred_element_type=F32)
    yield
    z = s * (1.0 / math.sqrt(HEAD_DIM)) + bias
    log_beta, log_1m = _log_terms(z)
    lbs = [log_beta[:, j * blk:(j + 1) * blk] for j in range(n_blocks)]
    l1s = [log_1m[:, j * blk:(j + 1) * blk] for j in range(n_blocks)]
    lbs[-1] = jnp.where(causal, lbs[-1], MASKED_LOG)
    l1s[-1] = jnp.where(causal, l1s[-1], 0.0)
    stacked = jnp.concatenate(l1s, axis=0)
    hi = stacked.astype(BF16)
    lo = (stacked - hi.astype(F32)).astype(BF16)
    yield
    excl = (jnp.dot(hi, tri, preferred_element_type=F32)
            + jnp.dot(lo, tri, preferred_element_type=F32))
    yield
    excl = [excl[j * blk:(j + 1) * blk] for j in range(n_blocks)]
    run = jnp.zeros((blk, 1), F32)
    ws = [None] * n_blocks
    for j in reversed(range(n_blocks)):
        ws[j] = jnp.exp(lbs[j] + excl[j] + run).astype(BF16)
        run = run + excl[j][:, :1] + l1s[j][:, :1]
    w = jnp.concatenate(ws, axis=1)
    yield
    out = jnp.dot(w, v_ref[0:n, :], preferred_element_type=F32)
    o_ref[a * blk:(a + 1) * blk, :] = out.astype(BF16)


def _sample_pages(q, bias, k_refs, v_refs, run_scr, acc_scr):
    page_cols = k_refs[0].shape[1]
    n_chunks = page_cols // LANE
    head = lax.broadcasted_iota(jnp.int32, (N_HEADS, page_cols), 0)
    col = lax.broadcasted_iota(jnp.int32, (N_HEADS, page_cols), 1)
    own = (col % N_HEADS) == head
    tri_ones = jnp.concatenate([_strict_lower(LANE), jnp.ones((LANE, LANE), BF16)], axis=1)
    scores = [lax.dot_general(q, k[0].astype(BF16), (((1,), (1,)), ((), ())),
                              preferred_element_type=F32) for k in k_refs]
    yield
    log_betas, stacked = [], []
    for s in scores:
        log_beta, log_1m = _log_terms(s * (1.0 / math.sqrt(HEAD_DIM)) + bias)
        log_1m = jnp.where(own, log_1m, 0.0)
        log_betas.append(log_beta)
        stacked += [log_1m[:, c * LANE:(c + 1) * LANE] for c in range(n_chunks)]
    stacked = jnp.concatenate(stacked, axis=0)
    hi = stacked.astype(BF16)
    lo = (stacked - hi.astype(F32)).astype(BF16)
    yield
    sums = (jnp.dot(hi, tri_ones, preferred_element_type=F32)
            + jnp.dot(lo, tri_ones, preferred_element_type=F32))
    yield
    run = run_scr[...]
    weights = []
    for r in range(len(k_refs)):
        later = [None] * n_chunks
        for c in reversed(range(n_chunks)):
            rows = sums[(r * n_chunks + c) * N_HEADS:(r * n_chunks + c + 1) * N_HEADS]
            later[c] = rows[:, :LANE] + run
            run = run + rows[:, LANE:]
        w = jnp.exp(log_betas[r] + jnp.concatenate(later, axis=1))
        weights.append(jnp.where(own, w, 0.0).astype(BF16))
    run_scr[...] = run
    yield
    acc = acc_scr[...]
    for w, v in zip(weights, v_refs):
        acc = acc + jnp.dot(w, v[0].astype(BF16), preferred_element_type=F32)
    acc_scr[...] = acc


def _interleave(generators, in_flight):
    waiting, active = list(generators), []
    while waiting or active:
        active = [g for g in active if next(g, "done") != "done"]
        if waiting and len(active) < in_flight:
            g = waiting.pop(0)
            next(g)
            active.append(g)


def _attn_kernel(pt_ref, bias_ref, q_ref, k_ref, v_ref, qs_ref, bias_col_ref, *refs,
                 blk, groups, pages_per_step, steps_per_seq, in_flight):
    del pt_ref
    k_pages = refs[:pages_per_step]
    v_pages = refs[pages_per_step:2 * pages_per_step]
    o_ref, os_ref, run_scr, acc_scr = refs[2 * pages_per_step:]
    group = pl.program_id(2)
    step = (pl.program_id(0) * pl.num_programs(1) + pl.program_id(1)) * groups + group
    within = lax.rem(step, steps_per_seq)
    n_blocks = q_ref.shape[0] // blk

    @pl.when(within == 0)
    def _():
        run_scr[...] = jnp.zeros_like(run_scr)
        acc_scr[...] = jnp.zeros_like(acc_scr)

    bias = bias_ref[pl.program_id(1)]
    tri = _strict_lower(blk)
    row = lax.broadcasted_iota(jnp.int32, (blk, blk), 0)
    col = lax.broadcasted_iota(jnp.int32, (blk, blk), 1)
    causal = col < row

    def body(p):
        mine = [a for a in reversed(range(n_blocks)) if a % (2 * groups) in (p, 2 * groups - 1 - p)]
        _interleave(
            [_sample_pages(qs_ref[0], bias_col_ref[...], k_pages, v_pages, run_scr, acc_scr)]
            + [_attn_rows(q_ref, k_ref, v_ref, o_ref, bias, tri, causal, a, blk) for a in mine],
            in_flight)

    for p in range(groups):
        pl.when(group == p)(functools.partial(body, p))

    @pl.when(within == steps_per_seq - 1)
    def _():
        os_ref[0] = acc_scr[...].astype(BF16)


def _attn(proj, sb_bias, batch, seq, q_s, cache_k, cache_v, layer, page_table, blk, groups, in_flight):
    db = q_s.shape[0]
    depth, n_pool, page, heads, d = cache_k.shape
    n_pages = page_table.shape[1]
    steps = batch * heads * groups
    assert (seq // blk) % (2 * groups) == 0 and (db * n_pages) % steps == 0
    pages_per_step = db * n_pages // steps
    assert n_pages % pages_per_step == 0
    steps_per_seq = n_pages // pages_per_step
    ck = cache_k.reshape(depth * n_pool, page * heads, d)
    cv = cache_v.reshape(depth * n_pool, page * heads, d)
    q_col, k_col, v_col = (2 * A_WIDTH) // LANE, (2 * A_WIDTH + SB_WIDTH) // LANE, (2 * A_WIDTH + 2 * SB_WIDTH) // LANE

    def head_spec(col0):
        return pl.BlockSpec((seq, HEAD_DIM), lambda b, h, p, pt, bias: (b, col0 + h))

    def seq_and_part(b, h, p):
        step = (b * heads + h) * groups + p
        return lax.div(step, steps_per_seq), lax.rem(step, steps_per_seq)

    def seq_spec():
        return pl.BlockSpec((1, heads, d), lambda b, h, p, pt, bias: (seq_and_part(b, h, p)[0], 0, 0))

    def page_spec(r):
        def index(b, h, p, pt, bias):
            s, part = seq_and_part(b, h, p)
            newest_first = n_pages - 1 - (part * pages_per_step + r)
            return (layer * n_pool + pt[s, newest_first], 0, 0)
        return pl.BlockSpec((1, page * heads, d), index)

    grid_spec = pltpu.PrefetchScalarGridSpec(
        num_scalar_prefetch=2,
        grid=(batch, heads, groups),
        in_specs=[head_spec(q_col), head_spec(k_col), head_spec(v_col), seq_spec(),
                  pl.BlockSpec((heads, 1), lambda b, h, p, pt, bias: (0, 0))]
        + [page_spec(r) for r in range(pages_per_step)] * 2,
        out_specs=[head_spec(0), seq_spec()],
        scratch_shapes=[pltpu.VMEM((heads, LANE), F32), pltpu.VMEM((heads, d), F32)],
    )
    return pl.pallas_call(
        functools.partial(_attn_kernel, blk=blk, groups=groups, pages_per_step=pages_per_step,
                          steps_per_seq=steps_per_seq, in_flight=in_flight),
        grid_spec=grid_spec,
        out_shape=[jax.ShapeDtypeStruct((batch * seq, SB_WIDTH), BF16),
                   jax.ShapeDtypeStruct((db, heads, d), BF16)],
        compiler_params=_params(("arbitrary", "arbitrary", "arbitrary")),
    )(page_table, sb_bias, proj, proj, proj, q_s, sb_bias.reshape(heads, 1),
      *([ck] * pages_per_step), *([cv] * pages_per_step))


def _mix_kernel(u_ref, va_ref, b_ref, ga0_ref, ga1_ref, gb0_ref, gb1_ref, x_ref, gt_ref, gpost_ref,
                ws_ref, bs_ref, wa_ref, wb_ref, wo_ref, o_ref, *, chunked):
    bm = u_ref.shape[0]
    u = u_ref[...].astype(F32)
    if chunked:
        row = lax.broadcasted_iota(jnp.int32, (CHUNK, CHUNK), 0)
        col = lax.broadcasted_iota(jnp.int32, (CHUNK, CHUNK), 1)
        keep = col <= row
        cols = []
        for g in range(A_GROUPS):
            w = jnp.where(keep, ws_ref[g], 0.0).astype(BF16)
            b = bs_ref[g]
            rows = []
            for c in range(bm // CHUNK):
                v = va_ref[c * CHUNK:(c + 1) * CHUNK, g * CHUNK:(g + 1) * CHUNK]
                rows.append(jnp.dot(w, v, preferred_element_type=F32) + b)
            cols.append(jnp.concatenate(rows, axis=0))
        s = jnp.concatenate(cols, axis=1)
    else:
        s = va_ref[...].astype(F32) * ws_ref[...] + bs_ref[...]
    a = (u * s).astype(BF16)
    p = jnp.dot(a, wa_ref[...], preferred_element_type=F32)
    q = jnp.dot(b_ref[...], wb_ref[...], preferred_element_type=F32)
    half = p.shape[1] // 2
    merged = jnp.concatenate([
        ga0_ref[...].astype(F32) * p[:, :half] + gb0_ref[...].astype(F32) * q[:, :half],
        ga1_ref[...].astype(F32) * p[:, half:] + gb1_ref[...].astype(F32) * q[:, half:],
    ], axis=1).astype(BF16)
    mix = jnp.dot(merged, wo_ref[...], preferred_element_type=F32)
    o_ref[...] = x_ref[...] + gt_ref[0] * _rms(mix, gpost_ref[...])


def _mix(proj, b_out, x, gt, g_post, ws, bs, wa, wb, wo, bm, chunked):
    m, d = x.shape
    tiles_per_group = m // bm // gt.shape[0]
    r = gt.shape[1]

    def col_spec(c):
        return pl.BlockSpec((bm, COL_TILE), lambda i: (i, c))

    def whole(a):
        return pl.BlockSpec(a.shape, lambda i: (0,) * a.ndim)

    return pl.pallas_call(
        functools.partial(_mix_kernel, chunked=chunked),
        grid=(m // bm,),
        in_specs=[
            col_spec(0), col_spec(1),
            pl.BlockSpec((bm, SB_WIDTH), lambda i: (i, 0)),
            col_spec(5), col_spec(6), col_spec(7), col_spec(8),
            pl.BlockSpec((bm, d), lambda i: (i, 0)),
            pl.BlockSpec((1, r, d), lambda i: (i // tiles_per_group, 0, 0)),
            pl.BlockSpec((1, d), lambda i: (0, 0)),
            whole(ws), whole(bs), whole(wa), whole(wb), whole(wo),
        ],
        out_specs=pl.BlockSpec((bm, d), lambda i: (i, 0)),
        out_shape=jax.ShapeDtypeStruct((m, d), F32),
        compiler_params=_params(("arbitrary",)),
    )(proj, proj, b_out, proj, proj, proj, proj, x, gt, g_post.reshape(1, d), ws, bs, wa, wb, wo)


def _ffn_kernel(x_ref, sc_ref, sh_ref, gt_ref, gpre_ref, gpost_ref, wg_ref, wu_ref, wo_ref, o_ref,
                h_scr, acc_scr):
    j = pl.program_id(1)
    last = pl.num_programs(1) - 1
    bm = x_ref.shape[0]
    row_blocks = [slice(r, r + min(bm, ROW_SPLIT)) for r in range(0, bm, min(bm, ROW_SPLIT))]

    def rows_of(ref, rs):
        return ref[0] if ref.shape[1] == 1 else ref[0, rs]

    def partial_out(h):
        gate = jnp.dot(h, wg_ref[...], preferred_element_type=F32)
        up = jnp.dot(h, wu_ref[...], preferred_element_type=F32)
        act = (gate * _sigmoid(gate) * up).astype(BF16)
        return jnp.dot(act, wo_ref[...], preferred_element_type=F32)

    @pl.when(j == 0)
    def _():
        for rs in row_blocks:
            h = (_rms(x_ref[rs, :], gpre_ref[...]) * (1.0 + rows_of(sc_ref, rs))
                 + rows_of(sh_ref, rs)).astype(BF16)
            h_scr[rs, :] = h
            acc_scr[rs, :] = partial_out(h)

    @pl.when(jnp.logical_and(j > 0, j < last))
    def _():
        acc_scr[...] += partial_out(h_scr[...])

    @pl.when(j == last)
    def _():
        for rs in row_blocks:
            f = acc_scr[rs, :] + partial_out(h_scr[rs, :])
            o_ref[rs, :] = x_ref[rs, :] + rows_of(gt_ref, rs) * _rms(f, gpost_ref[...])


def _ffn(x, sc, sh, gt, g_pre, g_post, w_ffn_in, w_ffn_out, bm, tf):
    m, d = x.shape
    d_ff = w_ffn_out.shape[0]
    nf = d_ff // tf
    assert nf >= 2, "the kernel treats the first and last d_ff steps separately"
    tiles_per_group = m // bm // sc.shape[0]
    r = sc.shape[1]
    mod_spec = pl.BlockSpec((1, r, d), lambda i, j: (i // tiles_per_group, 0, 0))
    vec_spec = pl.BlockSpec((1, d), lambda i, j: (0, 0))
    return pl.pallas_call(
        _ffn_kernel,
        grid=(m // bm, nf),
        in_specs=[
            pl.BlockSpec((bm, d), lambda i, j: (i, 0)),
            mod_spec, mod_spec, mod_spec, vec_spec, vec_spec,
            pl.BlockSpec((d, tf), lambda i, j: (0, j)),
            pl.BlockSpec((d, tf), lambda i, j: (0, nf + j)),
            pl.BlockSpec((tf, d), lambda i, j: (j, 0)),
        ],
        out_specs=pl.BlockSpec((bm, d), lambda i, j: (i, 0)),
        out_shape=jax.ShapeDtypeStruct((m, d), F32),
        scratch_shapes=[pltpu.VMEM((bm, d), BF16), pltpu.VMEM((bm, d), F32)],
        compiler_params=_params(("arbitrary", "arbitrary")),
    )(x, sc, sh, gt, g_pre.reshape(1, d), g_post.reshape(1, d), w_ffn_in, w_ffn_in, w_ffn_out)


def kernel(x_prompt, x_sample, c_prompt, c_sample, cache_k, cache_v, page_table, w_ada, b_ada, g_pre_mix, g_post_mix, w_in, ln_v_g, ln_v_b, w_s, b_s, sb_bias, w_branch_a, w_branch_b, w_out, g_pre_ffn, g_post_ffn, w_ffn_in, w_ffn_out):
    batch, seq, d = x_prompt.shape
    db, dseq, _ = x_sample.shape
    depth = w_ada.shape[0]
    assert dseq == 1, "the sample group decodes one token per sequence"
    mp, ms = batch * seq, db * dseq
    yp = x_prompt.reshape(mp, d)
    ys = x_sample.reshape(ms, d)
    c_all = jnp.concatenate([c_prompt, c_sample], axis=0)
    outs = {name: [] for name in ("kp", "vp", "ks", "vs", "cv")}

    for l in range(depth):
        w_in_l = w_in[l].astype(BF16)
        wa_l = w_branch_a[l].astype(BF16)
        wb_l = w_branch_b[l].astype(BF16)
        wo_l = w_out[l].astype(BF16)
        wfi_l = w_ffn_in[l].astype(BF16)
        wfo_l = w_ffn_out[l].astype(BF16)

        mods = _ada(c_all, w_ada[l], b_ada[l])
        mods_p = [m.reshape(batch, 1, d) for m in jnp.split(mods[:batch], 6, axis=-1)]
        mods_s = [m.reshape(1, db, d) for m in jnp.split(mods[batch:], 6, axis=-1)]

        sh1, sc1, gt1, sh2, sc2, gt2 = mods_p
        sh1_s, sc1_s, gt1_s, sh2_s, sc2_s, gt2_s = mods_s
        proj, kp, vp = _inproj(yp, sc1, sh1, g_pre_mix[l], w_in_l, ln_v_g[l], ln_v_b[l],
                               bm=1024, want_va=False)
        proj_s, k_s, v_s, cv_s = _inproj(ys, sc1_s, sh1_s, g_pre_mix[l], w_in_l, ln_v_g[l], ln_v_b[l],
                                         bm=ms, want_va=True)
        q_s = proj_s[:, 2 * A_WIDTH:2 * A_WIDTH + SB_WIDTH].reshape(ms, N_HEADS, HEAD_DIM)
        b_out, b_out_s = _attn(proj, sb_bias[l], batch, seq, q_s, cache_k, cache_v, l, page_table,
                               blk=256, groups=2, in_flight=4)

        x1 = _mix(proj, b_out, yp, gt1, g_post_mix[l], w_s[l], b_s[l].reshape(A_GROUPS, CHUNK, 1),
                  wa_l, wb_l, wo_l, bm=256, chunked=True)
        yp = _ffn(x1, sc2, sh2, gt2, g_pre_ffn[l], g_post_ffn[l], wfi_l, wfo_l, bm=1024, tf=256)
        outs["kp"].append(kp.reshape(batch, seq, N_HEADS, HEAD_DIM))
        outs["vp"].append(vp.reshape(batch, seq, N_HEADS, HEAD_DIM))

        ws_row = jnp.repeat(w_s[l][:, 0, 0], CHUNK).reshape(1, A_WIDTH)
        bs_row = jnp.repeat(b_s[l][:, 0], CHUNK).reshape(1, A_WIDTH)
        x1 = _mix(proj_s, b_out_s.reshape(ms, SB_WIDTH), ys, gt1_s, g_post_mix[l], ws_row, bs_row,
                  wa_l, wb_l, wo_l, bm=ms, chunked=False)
        ys = _ffn(x1, sc2_s, sh2_s, gt2_s, g_pre_ffn[l], g_post_ffn[l], wfi_l, wfo_l, bm=ms, tf=512)
        outs["ks"].append(k_s.reshape(db, dseq, N_HEADS, HEAD_DIM))
        outs["vs"].append(v_s.reshape(db, dseq, N_HEADS, HEAD_DIM))
        outs["cv"].append(cv_s.reshape(db, dseq, A_WIDTH))

    return (yp.reshape(batch, seq, d), ys.reshape(db, dseq, d),
            jnp.stack(outs["kp"]), jnp.stack(outs["vp"]),
            jnp.stack(outs["ks"]), jnp.stack(outs["vs"]), jnp.stack(outs["cv"]))
```

```python
import functools
import math

import jax
import jax.numpy as jnp
from jax import lax
from jax.experimental import pallas as pl
from jax.experimental.pallas import tpu as pltpu

F32 = jnp.float32
BF16 = jnp.bfloat16

EPS = 1e-6
LANE = 128
HEAD_DIM = 128
N_HEADS = 8
A_GROUPS = 8
CHUNK = 128
A_WIDTH = A_GROUPS * CHUNK
SB_WIDTH = N_HEADS * HEAD_DIM
COL_TILE = 1024
MXU_COLS = 256
ROW_SPLIT = 256
VMEM_LIMIT = 60 * 1024 * 1024


def _params(semantics):
    return pltpu.CompilerParams(dimension_semantics=semantics, vmem_limit_bytes=VMEM_LIMIT)


def _rms(x, g):
    return x * lax.rsqrt(jnp.mean(x * x, axis=-1, keepdims=True) + EPS) * g


def _sigmoid(x):
    return 1.0 / (1.0 + jnp.exp(-x))


def _ada_kernel(c_ref, w_ref, b_ref, o_ref):
    c = c_ref[...]
    a = (c * _sigmoid(c)).astype(BF16)
    o_ref[...] = jnp.dot(a, w_ref[...].astype(BF16), preferred_element_type=F32) + b_ref[...]


def _ada(c_all, w_ada, b_ada):
    rows, d = c_all.shape
    n = w_ada.shape[1]
    tn = 1024
    return pl.pallas_call(
        _ada_kernel,
        grid=(n // tn,),
        in_specs=[
            pl.BlockSpec((rows, d), lambda j: (0, 0)),
            pl.BlockSpec((d, tn), lambda j: (0, j)),
            pl.BlockSpec((1, tn), lambda j: (0, j)),
        ],
        out_specs=pl.BlockSpec((rows, tn), lambda j: (0, j)),
        out_shape=jax.ShapeDtypeStruct((rows, n), F32),
        compiler_params=_params(("arbitrary",)),
    )(c_all, w_ada, b_ada.reshape(1, n))


def _inproj_kernel(x_ref, sc_ref, sh_ref, g_ref, w_ref, lng_ref, lnb_ref,
                   proj_ref, k_ref, v_ref, *rest):
    va_ref, h_scr = rest if len(rest) == 2 else (None, rest[0])
    j = pl.program_id(1)
    bm = x_ref.shape[0]
    col_chunks = [slice(c, c + MXU_COLS) for c in range(0, COL_TILE, MXU_COLS)]

    def weight_chunks():
        return [w_ref[:, cs] for cs in col_chunks]

    def chunk(h, w):
        return jnp.dot(h, w, preferred_element_type=F32)

    @pl.when(j == 0)
    def _():
        ws = weight_chunks()
        for r in range(0, bm, min(bm, ROW_SPLIT)):
            rs = slice(r, r + min(bm, ROW_SPLIT))
            sc = sc_ref[0] if sc_ref.shape[1] == 1 else sc_ref[0, rs]
            sh = sh_ref[0] if sh_ref.shape[1] == 1 else sh_ref[0, rs]
            h = (_rms(x_ref[rs, :], g_ref[...]) * (1.0 + sc) + sh).astype(BF16)
            h_scr[rs, :] = h
            for cs, w in zip(col_chunks, ws):
                proj_ref[rs, cs] = jax.nn.gelu(chunk(h, w)).astype(BF16)

    @pl.when(j == 1)
    def _():
        ws = weight_chunks()
        for r in range(0, bm, min(bm, ROW_SPLIT)):
            rs = slice(r, r + min(bm, ROW_SPLIT))
            h = h_scr[rs, :]
            gs = [jax.nn.gelu(chunk(h, w)) for w in ws]
            mean = sum(jnp.sum(g, axis=-1, keepdims=True) for g in gs) * (1.0 / COL_TILE)
            xcs = [g - mean for g in gs]
            var = sum(jnp.sum(xc * xc, axis=-1, keepdims=True) for xc in xcs) * (1.0 / COL_TILE)
            inv = lax.rsqrt(var + EPS)
            for cs, xc in zip(col_chunks, xcs):
                y = xc * inv * lng_ref[:, cs] + lnb_ref[:, cs]
                if va_ref is not None:
                    va_ref[rs, cs] = y
                proj_ref[rs, cs] = y.astype(BF16)

    def raw(f32_ref):
        h = h_scr[...]
        for cs, w in zip(col_chunks, weight_chunks()):
            acc = chunk(h, w)
            if f32_ref is not None:
                f32_ref[:, cs] = acc
            proj_ref[:, cs] = acc.astype(BF16)

    pl.when(j == 2)(lambda: raw(None))
    pl.when(j == 3)(lambda: raw(k_ref))
    pl.when(j == 4)(lambda: raw(v_ref))

    @pl.when(j >= 5)
    def _():
        h = h_scr[...]
        for cs, w in zip(col_chunks, weight_chunks()):
            proj_ref[:, cs] = _sigmoid(chunk(h, w)).astype(BF16)


def _inproj(x, sc, sh, g_pre, w_in, ln_g, ln_b, bm, want_va):
    m, d = x.shape
    n_in = w_in.shape[1]
    tiles_per_group = m // bm // sc.shape[0]
    r = sc.shape[1]
    mod_spec = pl.BlockSpec((1, r, d), lambda i, j: (i // tiles_per_group, 0, 0))
    row_spec = pl.BlockSpec((bm, COL_TILE), lambda i, j: (i, 0))
    n_f32 = 3 if want_va else 2
    return pl.pallas_call(
        _inproj_kernel,
        grid=(m // bm, n_in // COL_TILE),
        in_specs=[
            pl.BlockSpec((bm, d), lambda i, j: (i, 0)),
            mod_spec, mod_spec,
            pl.BlockSpec((1, d), lambda i, j: (0, 0)),
            pl.BlockSpec((d, COL_TILE), lambda i, j: (0, j)),
            pl.BlockSpec((1, A_WIDTH), lambda i, j: (0, 0)),
            pl.BlockSpec((1, A_WIDTH), lambda i, j: (0, 0)),
        ],
        out_specs=[pl.BlockSpec((bm, COL_TILE), lambda i, j: (i, j))] + [row_spec] * n_f32,
        out_shape=[jax.ShapeDtypeStruct((m, n_in), BF16)]
        + [jax.ShapeDtypeStruct((m, COL_TILE), F32)] * n_f32,
        scratch_shapes=[pltpu.VMEM((bm, d), BF16)],
        compiler_params=_params(("arbitrary", "arbitrary")),
    )(x, sc, sh, g_pre.reshape(1, d), w_in, ln_g.reshape(1, A_WIDTH), ln_b.reshape(1, A_WIDTH))


def _log_terms(z):
    soft = jnp.log(1.0 + jnp.exp(-jnp.abs(z)))
    log_beta = jnp.minimum(z, 0.0) - soft
    return log_beta, log_beta - z


def _strict_lower(n):
    row = lax.broadcasted_iota(jnp.int32, (n, n), 0)
    col = lax.broadcasted_iota(jnp.int32, (n, n), 1)
    return (row > col).astype(BF16)


MASKED_LOG = -1e30


def _attn_rows(q_ref, k_ref, v_ref, o_ref, bias, tri, causal, a, blk):
    n_blocks = a + 1
    n = n_blocks * blk
    q = q_ref[a * blk:(a + 1) * blk, :]
    s = lax.dot_general(q, k_ref[0:n, :], (((1,), (1,)), ((), ())), preferred_element_type=F32)
    yield
    z = s * (1.0 / math.sqrt(HEAD_DIM)) + bias
    log_beta, log_1m = _log_terms(z)
    lbs = [log_beta[:, j * blk:(j + 1) * blk] for j in range(n_blocks)]
    l1s = [log_1m[:, j * blk:(j + 1) * blk] for j in range(n_blocks)]
    lbs[-1] = jnp.where(causal, lbs[-1], MASKED_LOG)
    l1s[-1] = jnp.where(causal, l1s[-1], 0.0)
    stacked = jnp.concatenate(l1s, axis=0).astype(BF16)
    yield
    excl = jnp.dot(stacked, tri, preferred_element_type=F32)
    yield
    excl = [excl[j * blk:(j + 1) * blk] for j in range(n_blocks)]
    run = jnp.zeros((blk, 1), F32)
    ws = [None] * n_blocks
    for j in reversed(range(n_blocks)):
        ws[j] = jnp.exp(lbs[j] + excl[j] + run).astype(BF16)
        run = run + excl[j][:, :1] + l1s[j][:, :1]
    w = jnp.concatenate(ws, axis=1)
    yield
    out = jnp.dot(w, v_ref[0:n, :], preferred_element_type=F32)
    o_ref[a * blk:(a + 1) * blk, :] = out.astype(BF16)


def _sample_pages(q, bias, k_refs, v_refs, run_scr, acc_scr):
    page_cols = k_refs[0].shape[0]
    n_chunks = page_cols // LANE
    head = lax.broadcasted_iota(jnp.int32, (N_HEADS, page_cols), 0)
    col = lax.broadcasted_iota(jnp.int32, (N_HEADS, page_cols), 1)
    own = (col % N_HEADS) == head
    tri_ones = jnp.concatenate([_strict_lower(LANE), jnp.ones((LANE, LANE), BF16)], axis=1)
    scores = [lax.dot_general(q, k[...].astype(BF16), (((1,), (1,)), ((), ())),
                              preferred_element_type=F32) for k in k_refs]
    yield
    log_betas, stacked = [], []
    for s in scores:
        log_beta, log_1m = _log_terms(s * (1.0 / math.sqrt(HEAD_DIM)) + bias)
        log_1m = jnp.where(own, log_1m, 0.0)
        log_betas.append(log_beta)
        stacked += [log_1m[:, c * LANE:(c + 1) * LANE] for c in range(n_chunks)]
    stacked = jnp.concatenate(stacked, axis=0).astype(BF16)
    yield
    sums = jnp.dot(stacked, tri_ones, preferred_element_type=F32)
    yield
    run = run_scr[...]
    weights = []
    for r in range(len(k_refs)):
        later = [None] * n_chunks
        for c in reversed(range(n_chunks)):
            rows = sums[(r * n_chunks + c) * N_HEADS:(r * n_chunks + c + 1) * N_HEADS]
            later[c] = rows[:, :LANE] + run
            run = run + rows[:, LANE:]
        w = jnp.exp(log_betas[r] + jnp.concatenate(later, axis=1))
        weights.append(jnp.where(own, w, 0.0).astype(BF16))
    run_scr[...] = run
    yield
    acc = acc_scr[...]
    for w, v in zip(weights, v_refs):
        acc = acc + jnp.dot(w, v[...].astype(BF16), preferred_element_type=F32)
    acc_scr[...] = acc


def _interleave(generators, in_flight):
    waiting, active = list(generators), []
    while waiting or active:
        active = [g for g in active if next(g, "done") != "done"]
        if waiting and len(active) < in_flight:
            g = waiting.pop(0)
            next(g)
            active.append(g)


def _attn_kernel(pt_ref, bias_ref, q_ref, k_ref, v_ref, qs_ref, bias_col_ref, ck_ref, cv_ref,
                 o_ref, os_ref, k_buf, v_buf, sem, run_scr, acc_scr,
                 *, blk, groups, steps_per_seq, first_page, in_flight):
    pages_per_step = k_buf.shape[1]
    n_pages = pages_per_step * steps_per_seq
    group = pl.program_id(2)
    step = (pl.program_id(0) * pl.num_programs(1) + pl.program_id(1)) * groups + group
    n_steps = pl.num_programs(0) * pl.num_programs(1) * groups
    within = lax.rem(step, steps_per_seq)
    slot = lax.rem(step, 2)
    n_blocks = q_ref.shape[0] // blk

    def page_copies(of_step, to_slot):
        seq_id = lax.div(of_step, steps_per_seq)
        part = lax.rem(of_step, steps_per_seq)
        copies = []
        for r in range(pages_per_step):
            page = first_page + pt_ref[seq_id, n_pages - 1 - (part * pages_per_step + r)]
            copies.append((pltpu.make_async_copy(ck_ref.at[page], k_buf.at[to_slot, r], sem.at[to_slot, 0]),
                           pltpu.make_async_copy(cv_ref.at[page], v_buf.at[to_slot, r], sem.at[to_slot, 1])))
        return copies

    def start(copies):
        for k_copy, v_copy in copies:
            k_copy.start(priority=0)
            v_copy.start(priority=1)

    @pl.when(step == 0)
    def _():
        start(page_copies(step, slot))

    @pl.when(step + 1 < n_steps)
    def _():
        start(page_copies(step + 1, 1 - slot))

    for k_copy, v_copy in page_copies(step, slot):
        k_copy.wait()
        v_copy.wait()
    k_pages = [k_buf.at[slot, r] for r in range(pages_per_step)]
    v_pages = [v_buf.at[slot, r] for r in range(pages_per_step)]

    @pl.when(within == 0)
    def _():
        run_scr[...] = jnp.zeros_like(run_scr)
        acc_scr[...] = jnp.zeros_like(acc_scr)

    bias = bias_ref[pl.program_id(1)]
    tri = _strict_lower(blk)
    row = lax.broadcasted_iota(jnp.int32, (blk, blk), 0)
    col = lax.broadcasted_iota(jnp.int32, (blk, blk), 1)
    causal = col < row

    def body(p):
        mine = [a for a in reversed(range(n_blocks)) if a % (2 * groups) in (p, 2 * groups - 1 - p)]
        _interleave(
            [_sample_pages(qs_ref[0], bias_col_ref[...], k_pages, v_pages, run_scr, acc_scr)]
            + [_attn_rows(q_ref, k_ref, v_ref, o_ref, bias, tri, causal, a, blk) for a in mine],
            in_flight)

    for p in range(groups):
        pl.when(group == p)(functools.partial(body, p))

    @pl.when(within == steps_per_seq - 1)
    def _():
        os_ref[0] = acc_scr[...].astype(BF16)


def _attn(proj, sb_bias, batch, seq, q_s, cache_k, cache_v, layer, page_table, blk, groups, in_flight):
    db = q_s.shape[0]
    depth, n_pool, page, heads, d = cache_k.shape
    n_pages = page_table.shape[1]
    steps = batch * heads * groups
    assert (seq // blk) % (2 * groups) == 0 and (db * n_pages) % steps == 0
    pages_per_step = db * n_pages // steps
    assert n_pages % pages_per_step == 0
    steps_per_seq = n_pages // pages_per_step
    ck = cache_k.reshape(depth * n_pool, page * heads, d)
    cv = cache_v.reshape(depth * n_pool, page * heads, d)
    q_col, k_col, v_col = (2 * A_WIDTH) // LANE, (2 * A_WIDTH + SB_WIDTH) // LANE, (2 * A_WIDTH + 2 * SB_WIDTH) // LANE

    def head_spec(col0):
        return pl.BlockSpec((seq, HEAD_DIM), lambda b, h, p, pt, bias: (b, col0 + h))

    def seq_spec():
        return pl.BlockSpec((1, heads, d), lambda b, h, p, pt, bias:
                            (lax.div((b * heads + h) * groups + p, steps_per_seq), 0, 0))

    page_buf = pltpu.VMEM((2, pages_per_step, page * heads, d), F32)
    grid_spec = pltpu.PrefetchScalarGridSpec(
        num_scalar_prefetch=2,
        grid=(batch, heads, groups),
        in_specs=[head_spec(q_col), head_spec(k_col), head_spec(v_col), seq_spec(),
                  pl.BlockSpec((heads, 1), lambda b, h, p, pt, bias: (0, 0)),
                  pl.BlockSpec(memory_space=pl.ANY), pl.BlockSpec(memory_space=pl.ANY)],
        out_specs=[head_spec(0), seq_spec()],
        scratch_shapes=[page_buf, page_buf, pltpu.SemaphoreType.DMA((2, 2)),
                        pltpu.VMEM((heads, LANE), F32), pltpu.VMEM((heads, d), F32)],
    )
    return pl.pallas_call(
        functools.partial(_attn_kernel, blk=blk, groups=groups, steps_per_seq=steps_per_seq,
                          first_page=layer * n_pool, in_flight=in_flight),
        grid_spec=grid_spec,
        out_shape=[jax.ShapeDtypeStruct((batch * seq, SB_WIDTH), BF16),
                   jax.ShapeDtypeStruct((db, heads, d), BF16)],
        compiler_params=_params(("arbitrary", "arbitrary", "arbitrary")),
    )(page_table, sb_bias, proj, proj, proj, q_s, sb_bias.reshape(heads, 1), ck, cv)


def _mix_kernel(u_ref, va_ref, b_ref, ga0_ref, ga1_ref, gb0_ref, gb1_ref, x_ref, gt_ref, gpost_ref,
                ws_ref, bs_ref, wa_ref, wb_ref, wo_ref, o_ref, *, chunked):
    bm = u_ref.shape[0]
    u = u_ref[...].astype(F32)
    if chunked:
        row = lax.broadcasted_iota(jnp.int32, (CHUNK, CHUNK), 0)
        col = lax.broadcasted_iota(jnp.int32, (CHUNK, CHUNK), 1)
        keep = col <= row
        cols = []
        for g in range(A_GROUPS):
            w = jnp.where(keep, ws_ref[g], 0.0).astype(BF16)
            b = bs_ref[g]
            rows = []
            for c in range(bm // CHUNK):
                v = va_ref[c * CHUNK:(c + 1) * CHUNK, g * CHUNK:(g + 1) * CHUNK]
                rows.append(jnp.dot(w, v, preferred_element_type=F32) + b)
            cols.append(jnp.concatenate(rows, axis=0))
        s = jnp.concatenate(cols, axis=1)
    else:
        s = va_ref[...].astype(F32) * ws_ref[...] + bs_ref[...]
    a = (u * s).astype(BF16)
    p = jnp.dot(a, wa_ref[...], preferred_element_type=F32)
    q = jnp.dot(b_ref[...], wb_ref[...], preferred_element_type=F32)
    half = p.shape[1] // 2
    merged = jnp.concatenate([
        ga0_ref[...].astype(F32) * p[:, :half] + gb0_ref[...].astype(F32) * q[:, :half],
        ga1_ref[...].astype(F32) * p[:, half:] + gb1_ref[...].astype(F32) * q[:, half:],
    ], axis=1).astype(BF16)
    mix = jnp.dot(merged, wo_ref[...], preferred_element_type=F32)
    o_ref[...] = x_ref[...] + gt_ref[0] * _rms(mix, gpost_ref[...])


def _mix(proj, b_out, x, gt, g_post, ws, bs, wa, wb, wo, bm, chunked):
    m, d = x.shape
    tiles_per_group = m // bm // gt.shape[0]
    r = gt.shape[1]

    def col_spec(c):
        return pl.BlockSpec((bm, COL_TILE), lambda i: (i, c))

    def whole(a):
        return pl.BlockSpec(a.shape, lambda i: (0,) * a.ndim)

    return pl.pallas_call(
        functools.partial(_mix_kernel, chunked=chunked),
        grid=(m // bm,),
        in_specs=[
            col_spec(0), col_spec(1),
            pl.BlockSpec((bm, SB_WIDTH), lambda i: (i, 0)),
            col_spec(5), col_spec(6), col_spec(7), col_spec(8),
            pl.BlockSpec((bm, d), lambda i: (i, 0)),
            pl.BlockSpec((1, r, d), lambda i: (i // tiles_per_group, 0, 0)),
            pl.BlockSpec((1, d), lambda i: (0, 0)),
            whole(ws), whole(bs), whole(wa), whole(wb), whole(wo),
        ],
        out_specs=pl.BlockSpec((bm, d), lambda i: (i, 0)),
        out_shape=jax.ShapeDtypeStruct((m, d), F32),
        compiler_params=_params(("arbitrary",)),
    )(proj, proj, b_out, proj, proj, proj, proj, x, gt, g_post.reshape(1, d), ws, bs, wa, wb, wo)


def _ffn_kernel(x_ref, sc_ref, sh_ref, gt_ref, gpre_ref, gpost_ref, wg_ref, wu_ref, wo_ref, o_ref,
                h_scr, acc_scr):
    j = pl.program_id(1)
    last = pl.num_programs(1) - 1
    bm = x_ref.shape[0]
    row_blocks = [slice(r, r + min(bm, ROW_SPLIT)) for r in range(0, bm, min(bm, ROW_SPLIT))]

    def rows_of(ref, rs):
        return ref[0] if ref.shape[1] == 1 else ref[0, rs]

    def weights():
        return wg_ref[...], wu_ref[...], wo_ref[...]

    def partial_out(h, w):
        gate = jnp.dot(h, w[0], preferred_element_type=F32)
        up = jnp.dot(h, w[1], preferred_element_type=F32)
        act = (gate * _sigmoid(gate) * up).astype(BF16)
        return jnp.dot(act, w[2], preferred_element_type=F32)

    @pl.when(j == 0)
    def _():
        w = weights()
        for rs in row_blocks:
            h = (_rms(x_ref[rs, :], gpre_ref[...]) * (1.0 + rows_of(sc_ref, rs))
                 + rows_of(sh_ref, rs)).astype(BF16)
            h_scr[rs, :] = h
            acc_scr[rs, :] = partial_out(h, w)

    @pl.when(jnp.logical_and(j > 0, j < last))
    def _():
        acc_scr[...] += partial_out(h_scr[...], weights())

    @pl.when(j == last)
    def _():
        w = weights()
        for rs in row_blocks:
            f = acc_scr[rs, :] + partial_out(h_scr[rs, :], w)
            o_ref[rs, :] = x_ref[rs, :] + rows_of(gt_ref, rs) * _rms(f, gpost_ref[...])


def _ffn(x, sc, sh, gt, g_pre, g_post, w_ffn_in, w_ffn_out, bm, tf):
    m, d = x.shape
    d_ff = w_ffn_out.shape[0]
    nf = d_ff // tf
    assert nf >= 2, "the kernel treats the first and last d_ff steps separately"
    tiles_per_group = m // bm // sc.shape[0]
    r = sc.shape[1]
    mod_spec = pl.BlockSpec((1, r, d), lambda i, j: (i // tiles_per_group, 0, 0))
    vec_spec = pl.BlockSpec((1, d), lambda i, j: (0, 0))
    return pl.pallas_call(
        _ffn_kernel,
        grid=(m // bm, nf),
        in_specs=[
            pl.BlockSpec((bm, d), lambda i, j: (i, 0)),
            mod_spec, mod_spec, mod_spec, vec_spec, vec_spec,
            pl.BlockSpec((d, tf), lambda i, j: (0, j)),
            pl.BlockSpec((d, tf), lambda i, j: (0, nf + j)),
            pl.BlockSpec((tf, d), lambda i, j: (j, 0)),
        ],
        out_specs=pl.BlockSpec((bm, d), lambda i, j: (i, 0)),
        out_shape=jax.ShapeDtypeStruct((m, d), F32),
        scratch_shapes=[pltpu.VMEM((bm, d), BF16), pltpu.VMEM((bm, d), F32)],
        compiler_params=_params(("arbitrary", "arbitrary")),
    )(x, sc, sh, gt, g_pre.reshape(1, d), g_post.reshape(1, d), w_ffn_in, w_ffn_in, w_ffn_out)


def kernel(x_prompt, x_sample, c_prompt, c_sample, cache_k, cache_v, page_table, w_ada, b_ada, g_pre_mix, g_post_mix, w_in, ln_v_g, ln_v_b, w_s, b_s, sb_bias, w_branch_a, w_branch_b, w_out, g_pre_ffn, g_post_ffn, w_ffn_in, w_ffn_out):
    batch, seq, d = x_prompt.shape
    db, dseq, _ = x_sample.shape
    depth = w_ada.shape[0]
    assert dseq == 1, "the sample group decodes one token per sequence"
    mp, ms = batch * seq, db * dseq
    yp = x_prompt.reshape(mp, d)
    ys = x_sample.reshape(ms, d)
    c_all = jnp.concatenate([c_prompt, c_sample], axis=0)
    outs = {name: [] for name in ("kp", "vp", "ks", "vs", "cv")}

    for l in range(depth):
        w_in_l = w_in[l].astype(BF16)
        wfi_l = w_ffn_in[l].astype(BF16)
        wfo_l = w_ffn_out[l].astype(BF16)
        wa_l = w_branch_a[l].astype(BF16)
        wb_l = w_branch_b[l].astype(BF16)
        wo_l = w_out[l].astype(BF16)

        mods = _ada(c_all, w_ada[l], b_ada[l])
        mods_p = [m.reshape(batch, 1, d) for m in jnp.split(mods[:batch], 6, axis=-1)]
        mods_s = [m.reshape(1, db, d) for m in jnp.split(mods[batch:], 6, axis=-1)]

        sh1, sc1, gt1, sh2, sc2, gt2 = mods_p
        sh1_s, sc1_s, gt1_s, sh2_s, sc2_s, gt2_s = mods_s
        proj, kp, vp = _inproj(yp, sc1, sh1, g_pre_mix[l], w_in_l, ln_v_g[l], ln_v_b[l],
                               bm=1024, want_va=False)
        proj_s, k_s, v_s, cv_s = _inproj(ys, sc1_s, sh1_s, g_pre_mix[l], w_in_l, ln_v_g[l], ln_v_b[l],
                                         bm=ms, want_va=True)
        q_s = proj_s[:, 2 * A_WIDTH:2 * A_WIDTH + SB_WIDTH].reshape(ms, N_HEADS, HEAD_DIM)
        b_out, b_out_s = _attn(proj, sb_bias[l], batch, seq, q_s, cache_k, cache_v, l, page_table,
                               blk=256, groups=2, in_flight=4)

        x1 = _mix(proj, b_out, yp, gt1, g_post_mix[l], w_s[l], b_s[l].reshape(A_GROUPS, CHUNK, 1),
                  wa_l, wb_l, wo_l, bm=256, chunked=True)
        yp = _ffn(x1, sc2, sh2, gt2, g_pre_ffn[l], g_post_ffn[l], wfi_l, wfo_l, bm=1024, tf=256)
        outs["kp"].append(kp.reshape(batch, seq, N_HEADS, HEAD_DIM))
        outs["vp"].append(vp.reshape(batch, seq, N_HEADS, HEAD_DIM))

        ws_row = jnp.repeat(w_s[l][:, 0, 0], CHUNK).reshape(1, A_WIDTH)
        bs_row = jnp.repeat(b_s[l][:, 0], CHUNK).reshape(1, A_WIDTH)
        x1 = _mix(proj_s, b_out_s.reshape(ms, SB_WIDTH), ys, gt1_s, g_post_mix[l], ws_row, bs_row,
                  wa_l, wb_l, wo_l, bm=ms, chunked=False)
        ys = _ffn(x1, sc2_s, sh2_s, gt2_s, g_pre_ffn[l], g_post_ffn[l], wfi_l, wfo_l, bm=ms, tf=512)
        outs["ks"].append(k_s.reshape(db, dseq, N_HEADS, HEAD_DIM))
        outs["vs"].append(v_s.reshape(db, dseq, N_HEADS, HEAD_DIM))
        outs["cv"].append(cv_s.reshape(db, dseq, A_WIDTH))

    return (yp.reshape(batch, seq, d), ys.reshape(db, dseq, d),
            jnp.stack(outs["kp"]), jnp.stack(outs["vp"]),
            jnp.stack(outs["ks"]), jnp.stack(outs["vs"]), jnp.stack(outs["cv"]))
```

```python
import functools
import math

import jax
import jax.numpy as jnp
from jax import lax
from jax.experimental import pallas as pl
from jax.experimental.pallas import tpu as pltpu

F32 = jnp.float32
BF16 = jnp.bfloat16

EPS = 1e-6
LANE = 128
HEAD_DIM = 128
N_HEADS = 8
A_GROUPS = 8
CHUNK = 128
A_WIDTH = A_GROUPS * CHUNK
SB_WIDTH = N_HEADS * HEAD_DIM
COL_TILE = 1024
MXU_COLS = 256
ROW_SPLIT = 256
FFN_TILE = 256
VMEM_LIMIT = 60 * 1024 * 1024


def _params(semantics):
    return pltpu.CompilerParams(dimension_semantics=semantics, vmem_limit_bytes=VMEM_LIMIT)


def _rms(x, g):
    return x * lax.rsqrt(jnp.mean(x * x, axis=-1, keepdims=True) + EPS) * g


def _sigmoid(x):
    return 1.0 / (1.0 + jnp.exp(-x))


def _ada_kernel(c_ref, w_ref, b_ref, o_ref):
    c = c_ref[...]
    a = (c * _sigmoid(c)).astype(BF16)
    o_ref[...] = jnp.dot(a, w_ref[...].astype(BF16), preferred_element_type=F32) + b_ref[...]


def _ada(c_all, w_ada, b_ada):
    rows, d = c_all.shape
    n = w_ada.shape[1]
    tn = 1024
    return pl.pallas_call(
        _ada_kernel,
        grid=(n // tn,),
        in_specs=[
            pl.BlockSpec((rows, d), lambda j: (0, 0)),
            pl.BlockSpec((d, tn), lambda j: (0, j)),
            pl.BlockSpec((1, tn), lambda j: (0, j)),
        ],
        out_specs=pl.BlockSpec((rows, tn), lambda j: (0, j)),
        out_shape=jax.ShapeDtypeStruct((rows, n), F32),
        compiler_params=_params(("arbitrary",)),
    )(c_all, w_ada, b_ada.reshape(1, n))


def _inproj_kernel(x_ref, sc_ref, sh_ref, g_ref, w_ref, lng_ref, lnb_ref,
                   proj_ref, k_ref, v_ref, *rest):
    va_ref, h_scr = rest if len(rest) == 2 else (None, rest[0])
    j = pl.program_id(1)
    bm = x_ref.shape[0]
    col_chunks = [slice(c, c + MXU_COLS) for c in range(0, COL_TILE, MXU_COLS)]

    def weight_chunks():
        return [w_ref[0, :, cs] for cs in col_chunks]

    def chunk(h, w):
        return jnp.dot(h, w, preferred_element_type=F32)

    @pl.when(j == 0)
    def _():
        ws = weight_chunks()
        for r in range(0, bm, min(bm, ROW_SPLIT)):
            rs = slice(r, r + min(bm, ROW_SPLIT))
            sc = sc_ref[0] if sc_ref.shape[1] == 1 else sc_ref[0, rs]
            sh = sh_ref[0] if sh_ref.shape[1] == 1 else sh_ref[0, rs]
            h = (_rms(x_ref[rs, :], g_ref[...]) * (1.0 + sc) + sh).astype(BF16)
            h_scr[rs, :] = h
            for cs, w in zip(col_chunks, ws):
                proj_ref[rs, cs] = jax.nn.gelu(chunk(h, w)).astype(BF16)

    @pl.when(j == 1)
    def _():
        ws = weight_chunks()
        for r in range(0, bm, min(bm, ROW_SPLIT)):
            rs = slice(r, r + min(bm, ROW_SPLIT))
            h = h_scr[rs, :]
            gs = [jax.nn.gelu(chunk(h, w)) for w in ws]
            mean = sum(jnp.sum(g, axis=-1, keepdims=True) for g in gs) * (1.0 / COL_TILE)
            xcs = [g - mean for g in gs]
            var = sum(jnp.sum(xc * xc, axis=-1, keepdims=True) for xc in xcs) * (1.0 / COL_TILE)
            inv = lax.rsqrt(var + EPS)
            for cs, xc in zip(col_chunks, xcs):
                y = xc * inv * lng_ref[:, cs] + lnb_ref[:, cs]
                if va_ref is not None:
                    va_ref[rs, cs] = y
                proj_ref[rs, cs] = y.astype(BF16)

    def raw(f32_ref):
        h = h_scr[...]
        for cs, w in zip(col_chunks, weight_chunks()):
            acc = chunk(h, w)
            if f32_ref is not None:
                f32_ref[:, cs] = acc
            proj_ref[:, cs] = acc.astype(BF16)

    pl.when(j == 2)(lambda: raw(None))
    pl.when(j == 3)(lambda: raw(k_ref))
    pl.when(j == 4)(lambda: raw(v_ref))

    @pl.when(j >= 5)
    def _():
        h = h_scr[...]
        for cs, w in zip(col_chunks, weight_chunks()):
            proj_ref[:, cs] = _sigmoid(chunk(h, w)).astype(BF16)


def _inproj(x, sc, sh, g_pre, w_in, ln_g, ln_b, bm, want_va):
    m, d = x.shape
    n_in = w_in.shape[0] * COL_TILE
    tiles_per_group = m // bm // sc.shape[0]
    r = sc.shape[1]
    mod_spec = pl.BlockSpec((1, r, d), lambda i, j: (i // tiles_per_group, 0, 0))
    row_spec = pl.BlockSpec((bm, COL_TILE), lambda i, j: (i, 0))
    n_f32 = 3 if want_va else 2
    return pl.pallas_call(
        _inproj_kernel,
        grid=(m // bm, n_in // COL_TILE),
        in_specs=[
            pl.BlockSpec((bm, d), lambda i, j: (i, 0)),
            mod_spec, mod_spec,
            pl.BlockSpec((1, d), lambda i, j: (0, 0)),
            pl.BlockSpec((1, d, COL_TILE), lambda i, j: (j, 0, 0)),
            pl.BlockSpec((1, A_WIDTH), lambda i, j: (0, 0)),
            pl.BlockSpec((1, A_WIDTH), lambda i, j: (0, 0)),
        ],
        out_specs=[pl.BlockSpec((bm, COL_TILE), lambda i, j: (i, j))] + [row_spec] * n_f32,
        out_shape=[jax.ShapeDtypeStruct((m, n_in), BF16)]
        + [jax.ShapeDtypeStruct((m, COL_TILE), F32)] * n_f32,
        scratch_shapes=[pltpu.VMEM((bm, d), BF16)],
        compiler_params=_params(("arbitrary", "arbitrary")),
    )(x, sc, sh, g_pre.reshape(1, d), w_in, ln_g.reshape(1, A_WIDTH), ln_b.reshape(1, A_WIDTH))


def _log_terms(z):
    soft = jnp.log(1.0 + jnp.exp(-jnp.abs(z)))
    log_beta = jnp.minimum(z, 0.0) - soft
    return log_beta, log_beta - z


def _strict_lower(n):
    row = lax.broadcasted_iota(jnp.int32, (n, n), 0)
    col = lax.broadcasted_iota(jnp.int32, (n, n), 1)
    return (row > col).astype(BF16)


MASKED_LOG = -1e30


def _attn_rows(q_ref, k_ref, v_ref, o_ref, bias, tri, causal, a, blk):
    n_blocks = a + 1
    n = n_blocks * blk
    q = q_ref[a * blk:(a + 1) * blk, :]
    s = lax.dot_general(q, k_ref[0:n, :], (((1,), (1,)), ((), ())), preferred_element_type=F32)
    yield
    z = s * (1.0 / math.sqrt(HEAD_DIM)) + bias
    log_beta, log_1m = _log_terms(z)
    lbs = [log_beta[:, j * blk:(j + 1) * blk] for j in range(n_blocks)]
    l1s = [log_1m[:, j * blk:(j + 1) * blk] for j in range(n_blocks)]
    lbs[-1] = jnp.where(causal, lbs[-1], MASKED_LOG)
    l1s[-1] = jnp.where(causal, l1s[-1], 0.0)
    stacked = jnp.concatenate(l1s, axis=0).astype(BF16)
    yield
    excl = jnp.dot(stacked, tri, preferred_element_type=F32)
    yield
    excl = [excl[j * blk:(j + 1) * blk] for j in range(n_blocks)]
    run = jnp.zeros((blk, 1), F32)
    ws = [None] * n_blocks
    for j in reversed(range(n_blocks)):
        ws[j] = jnp.exp(lbs[j] + excl[j] + run).astype(BF16)
        run = run + excl[j][:, :1] + l1s[j][:, :1]
    w = jnp.concatenate(ws, axis=1)
    yield
    out = jnp.dot(w, v_ref[0:n, :], preferred_element_type=F32)
    o_ref[a * blk:(a + 1) * blk, :] = out.astype(BF16)


def _sample_pages(q, bias, k_refs, v_refs, run_scr, acc_scr):
    page_cols = k_refs[0].shape[0]
    n_chunks = page_cols // LANE
    head = lax.broadcasted_iota(jnp.int32, (N_HEADS, page_cols), 0)
    col = lax.broadcasted_iota(jnp.int32, (N_HEADS, page_cols), 1)
    own = (col % N_HEADS) == head
    tri_ones = jnp.concatenate([_strict_lower(LANE), jnp.ones((LANE, LANE), BF16)], axis=1)
    scores = [lax.dot_general(q, k[...].astype(BF16), (((1,), (1,)), ((), ())),
                              preferred_element_type=F32) for k in k_refs]
    yield
    log_betas, stacked = [], []
    for s in scores:
        log_beta, log_1m = _log_terms(s * (1.0 / math.sqrt(HEAD_DIM)) + bias)
        log_1m = jnp.where(own, log_1m, 0.0)
        log_betas.append(log_beta)
        stacked += [log_1m[:, c * LANE:(c + 1) * LANE] for c in range(n_chunks)]
    stacked = jnp.concatenate(stacked, axis=0).astype(BF16)
    yield
    sums = jnp.dot(stacked, tri_ones, preferred_element_type=F32)
    yield
    run = run_scr[...]
    weights = []
    for r in range(len(k_refs)):
        later = [None] * n_chunks
        for c in reversed(range(n_chunks)):
            rows = sums[(r * n_chunks + c) * N_HEADS:(r * n_chunks + c + 1) * N_HEADS]
            later[c] = rows[:, :LANE] + run
            run = run + rows[:, LANE:]
        w = jnp.exp(log_betas[r] + jnp.concatenate(later, axis=1))
        weights.append(jnp.where(own, w, 0.0).astype(BF16))
    run_scr[...] = run
    yield
    acc = acc_scr[...]
    for w, v in zip(weights, v_refs):
        acc = acc + jnp.dot(w, v[...].astype(BF16), preferred_element_type=F32)
    acc_scr[...] = acc


def _interleave(generators, in_flight):
    waiting, active = list(generators), []
    while waiting or active:
        active = [g for g in active if next(g, "done") != "done"]
        if waiting and len(active) < in_flight:
            g = waiting.pop(0)
            next(g)
            active.append(g)


def _attn_kernel(pt_ref, bias_ref, q_ref, k_ref, v_ref, qs_ref, bias_col_ref, ck_ref, cv_ref,
                 o_ref, os_ref, k_buf, v_buf, sem, run_scr, acc_scr,
                 *, blk, groups, steps_per_seq, first_page, in_flight):
    pages_per_step = k_buf.shape[1]
    n_pages = pages_per_step * steps_per_seq
    group = pl.program_id(2)
    step = (pl.program_id(0) * pl.num_programs(1) + pl.program_id(1)) * groups + group
    n_steps = pl.num_programs(0) * pl.num_programs(1) * groups
    within = lax.rem(step, steps_per_seq)
    slot = lax.rem(step, 2)
    n_blocks = q_ref.shape[0] // blk

    def page_copies(of_step, to_slot):
        seq_id = lax.div(of_step, steps_per_seq)
        part = lax.rem(of_step, steps_per_seq)
        copies = []
        for r in range(pages_per_step):
            page = first_page + pt_ref[seq_id, n_pages - 1 - (part * pages_per_step + r)]
            copies.append((pltpu.make_async_copy(ck_ref.at[page], k_buf.at[to_slot, r], sem.at[to_slot, 0]),
                           pltpu.make_async_copy(cv_ref.at[page], v_buf.at[to_slot, r], sem.at[to_slot, 1])))
        return copies

    def start(copies):
        for k_copy, v_copy in copies:
            k_copy.start(priority=0)
            v_copy.start(priority=1)

    @pl.when(step == 0)
    def _():
        start(page_copies(step, slot))

    @pl.when(step + 1 < n_steps)
    def _():
        start(page_copies(step + 1, 1 - slot))

    for k_copy, v_copy in page_copies(step, slot):
        k_copy.wait()
        v_copy.wait()
    k_pages = [k_buf.at[slot, r] for r in range(pages_per_step)]
    v_pages = [v_buf.at[slot, r] for r in range(pages_per_step)]

    @pl.when(within == 0)
    def _():
        run_scr[...] = jnp.zeros_like(run_scr)
        acc_scr[...] = jnp.zeros_like(acc_scr)

    bias = bias_ref[pl.program_id(1)]
    tri = _strict_lower(blk)
    row = lax.broadcasted_iota(jnp.int32, (blk, blk), 0)
    col = lax.broadcasted_iota(jnp.int32, (blk, blk), 1)
    causal = col < row

    def body(p):
        mine = [a for a in reversed(range(n_blocks)) if a % (2 * groups) in (p, 2 * groups - 1 - p)]
        _interleave(
            [_sample_pages(qs_ref[0], bias_col_ref[...], k_pages, v_pages, run_scr, acc_scr)]
            + [_attn_rows(q_ref, k_ref, v_ref, o_ref, bias, tri, causal, a, blk) for a in mine],
            in_flight)

    for p in range(groups):
        pl.when(group == p)(functools.partial(body, p))

    @pl.when(within == steps_per_seq - 1)
    def _():
        os_ref[0] = acc_scr[...].astype(BF16)


def _attn(proj, sb_bias, batch, seq, q_s, cache_k, cache_v, layer, page_table, blk, groups, in_flight):
    db = q_s.shape[0]
    depth, n_pool, page, heads, d = cache_k.shape
    n_pages = page_table.shape[1]
    steps = batch * heads * groups
    assert (seq // blk) % (2 * groups) == 0 and (db * n_pages) % steps == 0
    pages_per_step = db * n_pages // steps
    assert n_pages % pages_per_step == 0
    steps_per_seq = n_pages // pages_per_step
    ck = cache_k.reshape(depth * n_pool, page * heads, d)
    cv = cache_v.reshape(depth * n_pool, page * heads, d)
    q_col, k_col, v_col = (2 * A_WIDTH) // LANE, (2 * A_WIDTH + SB_WIDTH) // LANE, (2 * A_WIDTH + 2 * SB_WIDTH) // LANE

    def head_spec(col0):
        return pl.BlockSpec((seq, HEAD_DIM), lambda b, h, p, pt, bias: (b, col0 + h))

    def seq_spec():
        return pl.BlockSpec((1, heads, d), lambda b, h, p, pt, bias:
                            (lax.div((b * heads + h) * groups + p, steps_per_seq), 0, 0))

    page_buf = pltpu.VMEM((2, pages_per_step, page * heads, d), F32)
    grid_spec = pltpu.PrefetchScalarGridSpec(
        num_scalar_prefetch=2,
        grid=(batch, heads, groups),
        in_specs=[head_spec(q_col), head_spec(k_col), head_spec(v_col), seq_spec(),
                  pl.BlockSpec((heads, 1), lambda b, h, p, pt, bias: (0, 0)),
                  pl.BlockSpec(memory_space=pl.ANY), pl.BlockSpec(memory_space=pl.ANY)],
        out_specs=[head_spec(0), seq_spec()],
        scratch_shapes=[page_buf, page_buf, pltpu.SemaphoreType.DMA((2, 2)),
                        pltpu.VMEM((heads, LANE), F32), pltpu.VMEM((heads, d), F32)],
    )
    return pl.pallas_call(
        functools.partial(_attn_kernel, blk=blk, groups=groups, steps_per_seq=steps_per_seq,
                          first_page=layer * n_pool, in_flight=in_flight),
        grid_spec=grid_spec,
        out_shape=[jax.ShapeDtypeStruct((batch * seq, SB_WIDTH), BF16),
                   jax.ShapeDtypeStruct((db, heads, d), BF16)],
        compiler_params=_params(("arbitrary", "arbitrary", "arbitrary")),
    )(page_table, sb_bias, proj, proj, proj, q_s, sb_bias.reshape(heads, 1), ck, cv)


def _mix_kernel(u_ref, va_ref, b_ref, ga0_ref, ga1_ref, gb0_ref, gb1_ref, x_ref, gt_ref, gpost_ref,
                ws_ref, bs_ref, wa_ref, wb_ref, wo_ref, o_ref, *, chunked):
    bm = u_ref.shape[0]
    u = u_ref[...].astype(F32)
    if chunked:
        row = lax.broadcasted_iota(jnp.int32, (CHUNK, CHUNK), 0)
        col = lax.broadcasted_iota(jnp.int32, (CHUNK, CHUNK), 1)
        keep = col <= row
        cols = []
        for g in range(A_GROUPS):
            w = jnp.where(keep, ws_ref[g], 0.0).astype(BF16)
            b = bs_ref[g]
            rows = []
            for c in range(bm // CHUNK):
                v = va_ref[c * CHUNK:(c + 1) * CHUNK, g * CHUNK:(g + 1) * CHUNK]
                rows.append(jnp.dot(w, v, preferred_element_type=F32) + b)
            cols.append(jnp.concatenate(rows, axis=0))
        s = jnp.concatenate(cols, axis=1)
    else:
        s = va_ref[...].astype(F32) * ws_ref[...] + bs_ref[...]
    a = (u * s).astype(BF16)
    p = jnp.dot(a, wa_ref[...], preferred_element_type=F32)
    q = jnp.dot(b_ref[...], wb_ref[...], preferred_element_type=F32)
    half = p.shape[1] // 2
    merged = jnp.concatenate([
        ga0_ref[...].astype(F32) * p[:, :half] + gb0_ref[...].astype(F32) * q[:, :half],
        ga1_ref[...].astype(F32) * p[:, half:] + gb1_ref[...].astype(F32) * q[:, half:],
    ], axis=1).astype(BF16)
    mix = jnp.dot(merged, wo_ref[...], preferred_element_type=F32)
    o_ref[...] = x_ref[...] + gt_ref[0] * _rms(mix, gpost_ref[...])


def _mix(proj, b_out, x, gt, g_post, ws, bs, wa, wb, wo, bm, chunked):
    m, d = x.shape
    tiles_per_group = m // bm // gt.shape[0]
    r = gt.shape[1]

    def col_spec(c):
        return pl.BlockSpec((bm, COL_TILE), lambda i: (i, c))

    def whole(a):
        return pl.BlockSpec(a.shape, lambda i: (0,) * a.ndim)

    return pl.pallas_call(
        functools.partial(_mix_kernel, chunked=chunked),
        grid=(m // bm,),
        in_specs=[
            col_spec(0), col_spec(1),
            pl.BlockSpec((bm, SB_WIDTH), lambda i: (i, 0)),
            col_spec(5), col_spec(6), col_spec(7), col_spec(8),
            pl.BlockSpec((bm, d), lambda i: (i, 0)),
            pl.BlockSpec((1, r, d), lambda i: (i // tiles_per_group, 0, 0)),
            pl.BlockSpec((1, d), lambda i: (0, 0)),
            whole(ws), whole(bs), whole(wa), whole(wb), whole(wo),
        ],
        out_specs=pl.BlockSpec((bm, d), lambda i: (i, 0)),
        out_shape=jax.ShapeDtypeStruct((m, d), F32),
        compiler_params=_params(("arbitrary",)),
    )(proj, proj, b_out, proj, proj, proj, proj, x, gt, g_post.reshape(1, d), ws, bs, wa, wb, wo)


def _ffn_kernel(x_ref, sc_ref, sh_ref, gt_ref, gpre_ref, gpost_ref, wg_ref, wu_ref, wo_ref, o_ref,
                h_scr, acc_scr):
    j = pl.program_id(1)
    last = pl.num_programs(1) - 1
    bm = x_ref.shape[0]
    row_blocks = [slice(r, r + min(bm, ROW_SPLIT)) for r in range(0, bm, min(bm, ROW_SPLIT))]

    def rows_of(ref, rs):
        return ref[0] if ref.shape[1] == 1 else ref[0, rs]

    def weights():
        return wg_ref[0], wu_ref[0], wo_ref[...]

    def partial_out(h, w):
        gate = jnp.dot(h, w[0], preferred_element_type=F32)
        up = jnp.dot(h, w[1], preferred_element_type=F32)
        act = (gate * _sigmoid(gate) * up).astype(BF16)
        return jnp.dot(act, w[2], preferred_element_type=F32)

    @pl.when(j == 0)
    def _():
        w = weights()
        for rs in row_blocks:
            h = (_rms(x_ref[rs, :], gpre_ref[...]) * (1.0 + rows_of(sc_ref, rs))
                 + rows_of(sh_ref, rs)).astype(BF16)
            h_scr[rs, :] = h
            acc_scr[rs, :] = partial_out(h, w)

    @pl.when(jnp.logical_and(j > 0, j < last))
    def _():
        acc_scr[...] += partial_out(h_scr[...], weights())

    @pl.when(j == last)
    def _():
        w = weights()
        for rs in row_blocks:
            f = acc_scr[rs, :] + partial_out(h_scr[rs, :], w)
            o_ref[rs, :] = x_ref[rs, :] + rows_of(gt_ref, rs) * _rms(f, gpost_ref[...])


def _ffn(x, sc, sh, gt, g_pre, g_post, w_ffn_in, w_ffn_out, bm):
    m, d = x.shape
    d_ff = w_ffn_out.shape[0]
    tf = w_ffn_in.shape[2]
    nf = d_ff // tf
    assert nf >= 2, "the kernel treats the first and last d_ff steps separately"
    tiles_per_group = m // bm // sc.shape[0]
    r = sc.shape[1]
    mod_spec = pl.BlockSpec((1, r, d), lambda i, j: (i // tiles_per_group, 0, 0))
    vec_spec = pl.BlockSpec((1, d), lambda i, j: (0, 0))
    return pl.pallas_call(
        _ffn_kernel,
        grid=(m // bm, nf),
        in_specs=[
            pl.BlockSpec((bm, d), lambda i, j: (i, 0)),
            mod_spec, mod_spec, mod_spec, vec_spec, vec_spec,
            pl.BlockSpec((1, d, tf), lambda i, j: (j, 0, 0)),
            pl.BlockSpec((1, d, tf), lambda i, j: (nf + j, 0, 0)),
            pl.BlockSpec((tf, d), lambda i, j: (j, 0)),
        ],
        out_specs=pl.BlockSpec((bm, d), lambda i, j: (i, 0)),
        out_shape=jax.ShapeDtypeStruct((m, d), F32),
        scratch_shapes=[pltpu.VMEM((bm, d), BF16), pltpu.VMEM((bm, d), F32)],
        compiler_params=_params(("arbitrary", "arbitrary")),
    )(x, sc, sh, gt, g_pre.reshape(1, d), g_post.reshape(1, d), w_ffn_in, w_ffn_in, w_ffn_out)


def _column_tiles(w, tile):
    k, n = w.shape
    return w.astype(BF16).reshape(k, n // tile, tile).transpose(1, 0, 2)


def kernel(x_prompt, x_sample, c_prompt, c_sample, cache_k, cache_v, page_table, w_ada, b_ada, g_pre_mix, g_post_mix, w_in, ln_v_g, ln_v_b, w_s, b_s, sb_bias, w_branch_a, w_branch_b, w_out, g_pre_ffn, g_post_ffn, w_ffn_in, w_ffn_out):
    batch, seq, d = x_prompt.shape
    db, dseq, _ = x_sample.shape
    depth = w_ada.shape[0]
    assert dseq == 1, "the sample group decodes one token per sequence"
    mp, ms = batch * seq, db * dseq
    yp = x_prompt.reshape(mp, d)
    ys = x_sample.reshape(ms, d)
    c_all = jnp.concatenate([c_prompt, c_sample], axis=0)
    outs = {name: [] for name in ("kp", "vp", "ks", "vs", "cv")}

    for l in range(depth):
        w_in_l = _column_tiles(w_in[l], COL_TILE)
        wfi_l = _column_tiles(w_ffn_in[l], FFN_TILE)
        wfo_l = w_ffn_out[l].astype(BF16)
        wa_l = w_branch_a[l].astype(BF16)
        wb_l = w_branch_b[l].astype(BF16)
        wo_l = w_out[l].astype(BF16)

        mods = _ada(c_all, w_ada[l], b_ada[l])
        mods_p = [m.reshape(batch, 1, d) for m in jnp.split(mods[:batch], 6, axis=-1)]
        mods_s = [m.reshape(1, db, d) for m in jnp.split(mods[batch:], 6, axis=-1)]

        sh1, sc1, gt1, sh2, sc2, gt2 = mods_p
        sh1_s, sc1_s, gt1_s, sh2_s, sc2_s, gt2_s = mods_s
        proj, kp, vp = _inproj(yp, sc1, sh1, g_pre_mix[l], w_in_l, ln_v_g[l], ln_v_b[l],
                               bm=1024, want_va=False)
        proj_s, k_s, v_s, cv_s = _inproj(ys, sc1_s, sh1_s, g_pre_mix[l], w_in_l, ln_v_g[l], ln_v_b[l],
                                         bm=ms, want_va=True)
        q_s = proj_s[:, 2 * A_WIDTH:2 * A_WIDTH + SB_WIDTH].reshape(ms, N_HEADS, HEAD_DIM)
        b_out, b_out_s = _attn(proj, sb_bias[l], batch, seq, q_s, cache_k, cache_v, l, page_table,
                               blk=256, groups=2, in_flight=4)

        x1 = _mix(proj, b_out, yp, gt1, g_post_mix[l], w_s[l], b_s[l].reshape(A_GROUPS, CHUNK, 1),
                  wa_l, wb_l, wo_l, bm=256, chunked=True)
        yp = _ffn(x1, sc2, sh2, gt2, g_pre_ffn[l], g_post_ffn[l], wfi_l, wfo_l, bm=1024)
        outs["kp"].append(kp.reshape(batch, seq, N_HEADS, HEAD_DIM))
        outs["vp"].append(vp.reshape(batch, seq, N_HEADS, HEAD_DIM))

        ws_row = jnp.repeat(w_s[l][:, 0, 0], CHUNK).reshape(1, A_WIDTH)
        bs_row = jnp.repeat(b_s[l][:, 0], CHUNK).reshape(1, A_WIDTH)
        x1 = _mix(proj_s, b_out_s.reshape(ms, SB_WIDTH), ys, gt1_s, g_post_mix[l], ws_row, bs_row,
                  wa_l, wb_l, wo_l, bm=ms, chunked=False)
        ys = _ffn(x1, sc2_s, sh2_s, gt2_s, g_pre_ffn[l], g_post_ffn[l], wfi_l, wfo_l, bm=ms)
        outs["ks"].append(k_s.reshape(db, dseq, N_HEADS, HEAD_DIM))
        outs["vs"].append(v_s.reshape(db, dseq, N_HEADS, HEAD_DIM))
        outs["cv"].append(cv_s.reshape(db, dseq, A_WIDTH))

    return (yp.reshape(batch, seq, d), ys.reshape(db, dseq, d),
            jnp.stack(outs["kp"]), jnp.stack(outs["vp"]),
            jnp.stack(outs["ks"]), jnp.stack(outs["vs"]), jnp.stack(outs["cv"]))
```

```python
import functools
import math

import jax
import jax.numpy as jnp
from jax import lax
from jax.experimental import pallas as pl
from jax.experimental.pallas import tpu as pltpu

F32 = jnp.float32
BF16 = jnp.bfloat16

EPS = 1e-6
LANE = 128
HEAD_DIM = 128
N_HEADS = 8
A_GROUPS = 8
CHUNK = 128
A_WIDTH = A_GROUPS * CHUNK
SB_WIDTH = N_HEADS * HEAD_DIM
COL_TILE = 1024
MXU_COLS = 256
ROW_SPLIT = 256
VMEM_LIMIT = 60 * 1024 * 1024


def _params(semantics):
    return pltpu.CompilerParams(dimension_semantics=semantics, vmem_limit_bytes=VMEM_LIMIT)


def _rms(x, g):
    return x * lax.rsqrt(jnp.mean(x * x, axis=-1, keepdims=True) + EPS) * g


def _sigmoid(x):
    return 1.0 / (1.0 + jnp.exp(-x))


def _ada_kernel(c_ref, w_ref, b_ref, o_ref):
    c = c_ref[...]
    a = (c * _sigmoid(c)).astype(BF16)
    o_ref[...] = jnp.dot(a, w_ref[...].astype(BF16), preferred_element_type=F32) + b_ref[...]


def _ada(c_all, w_ada, b_ada):
    rows, d = c_all.shape
    n = w_ada.shape[1]
    tn = 1024
    return pl.pallas_call(
        _ada_kernel,
        grid=(n // tn,),
        in_specs=[
            pl.BlockSpec((rows, d), lambda j: (0, 0)),
            pl.BlockSpec((d, tn), lambda j: (0, j)),
            pl.BlockSpec((1, tn), lambda j: (0, j)),
        ],
        out_specs=pl.BlockSpec((rows, tn), lambda j: (0, j)),
        out_shape=jax.ShapeDtypeStruct((rows, n), F32),
        compiler_params=_params(("arbitrary",)),
    )(c_all, w_ada, b_ada.reshape(1, n))


def _inproj_kernel(x_ref, sc_ref, sh_ref, g_ref, w_ref, lng_ref, lnb_ref,
                   proj_ref, k_ref, v_ref, *rest):
    va_ref, h_scr = rest if len(rest) == 2 else (None, rest[0])
    j = pl.program_id(1)
    bm = x_ref.shape[0]
    col_chunks = [slice(c, c + MXU_COLS) for c in range(0, COL_TILE, MXU_COLS)]

    def weight_chunks():
        return [w_ref[:, cs] for cs in col_chunks]

    def chunk(h, w):
        return jnp.dot(h, w, preferred_element_type=F32)

    @pl.when(j == 0)
    def _():
        ws = weight_chunks()
        for r in range(0, bm, min(bm, ROW_SPLIT)):
            rs = slice(r, r + min(bm, ROW_SPLIT))
            sc = sc_ref[0] if sc_ref.shape[1] == 1 else sc_ref[0, rs]
            sh = sh_ref[0] if sh_ref.shape[1] == 1 else sh_ref[0, rs]
            h = (_rms(x_ref[rs, :], g_ref[...]) * (1.0 + sc) + sh).astype(BF16)
            h_scr[rs, :] = h
            for cs, w in zip(col_chunks, ws):
                proj_ref[rs, cs] = jax.nn.gelu(chunk(h, w)).astype(BF16)

    @pl.when(j == 1)
    def _():
        ws = weight_chunks()
        for r in range(0, bm, min(bm, ROW_SPLIT)):
            rs = slice(r, r + min(bm, ROW_SPLIT))
            h = h_scr[rs, :]
            gs = [jax.nn.gelu(chunk(h, w)) for w in ws]
            mean = sum(jnp.sum(g, axis=-1, keepdims=True) for g in gs) * (1.0 / COL_TILE)
            xcs = [g - mean for g in gs]
            var = sum(jnp.sum(xc * xc, axis=-1, keepdims=True) for xc in xcs) * (1.0 / COL_TILE)
            inv = lax.rsqrt(var + EPS)
            for cs, xc in zip(col_chunks, xcs):
                y = xc * inv * lng_ref[:, cs] + lnb_ref[:, cs]
                if va_ref is not None:
                    va_ref[rs, cs] = y
                proj_ref[rs, cs] = y.astype(BF16)

    def raw(f32_ref):
        h = h_scr[...]
        for cs, w in zip(col_chunks, weight_chunks()):
            acc = chunk(h, w)
            if f32_ref is not None:
                f32_ref[:, cs] = acc
            proj_ref[:, cs] = acc.astype(BF16)

    pl.when(j == 2)(lambda: raw(None))
    pl.when(j == 3)(lambda: raw(k_ref))
    pl.when(j == 4)(lambda: raw(v_ref))

    @pl.when(j >= 5)
    def _():
        h = h_scr[...]
        for cs, w in zip(col_chunks, weight_chunks()):
            proj_ref[:, cs] = _sigmoid(chunk(h, w)).astype(BF16)


def _inproj(x, sc, sh, g_pre, w_in, ln_g, ln_b, bm, want_va):
    m, d = x.shape
    n_in = w_in.shape[1]
    tiles_per_group = m // bm // sc.shape[0]
    r = sc.shape[1]
    mod_spec = pl.BlockSpec((1, r, d), lambda i, j: (i // tiles_per_group, 0, 0))
    row_spec = pl.BlockSpec((bm, COL_TILE), lambda i, j: (i, 0))
    n_f32 = 3 if want_va else 2
    return pl.pallas_call(
        _inproj_kernel,
        grid=(m // bm, n_in // COL_TILE),
        in_specs=[
            pl.BlockSpec((bm, d), lambda i, j: (i, 0)),
            mod_spec, mod_spec,
            pl.BlockSpec((1, d), lambda i, j: (0, 0)),
            pl.BlockSpec((d, COL_TILE), lambda i, j: (0, j)),
            pl.BlockSpec((1, A_WIDTH), lambda i, j: (0, 0)),
            pl.BlockSpec((1, A_WIDTH), lambda i, j: (0, 0)),
        ],
        out_specs=[pl.BlockSpec((bm, COL_TILE), lambda i, j: (i, j))] + [row_spec] * n_f32,
        out_shape=[jax.ShapeDtypeStruct((m, n_in), BF16)]
        + [jax.ShapeDtypeStruct((m, COL_TILE), F32)] * n_f32,
        scratch_shapes=[pltpu.VMEM((bm, d), BF16)],
        compiler_params=_params(("arbitrary", "arbitrary")),
    )(x, sc, sh, g_pre.reshape(1, d), w_in, ln_g.reshape(1, A_WIDTH), ln_b.reshape(1, A_WIDTH))


def _log_terms(z):
    soft = jnp.log(1.0 + jnp.exp(-jnp.abs(z)))
    log_beta = jnp.minimum(z, 0.0) - soft
    return log_beta, log_beta - z


def _strict_lower(n):
    row = lax.broadcasted_iota(jnp.int32, (n, n), 0)
    col = lax.broadcasted_iota(jnp.int32, (n, n), 1)
    return (row > col).astype(BF16)


MASKED_LOG = -1e30


def _attn_rows(q_ref, k_ref, v_ref, o_ref, bias, tri, causal, a, blk):
    n_blocks = a + 1
    n = n_blocks * blk
    q = q_ref[a * blk:(a + 1) * blk, :]
    s = lax.dot_general(q, k_ref[0:n, :], (((1,), (1,)), ((), ())), preferred_element_type=F32)
    yield
    z = s * (1.0 / math.sqrt(HEAD_DIM)) + bias
    log_beta, log_1m = _log_terms(z)
    lbs = [log_beta[:, j * blk:(j + 1) * blk] for j in range(n_blocks)]
    l1s = [log_1m[:, j * blk:(j + 1) * blk] for j in range(n_blocks)]
    lbs[-1] = jnp.where(causal, lbs[-1], MASKED_LOG)
    l1s[-1] = jnp.where(causal, l1s[-1], 0.0)
    stacked = jnp.concatenate(l1s, axis=0).astype(BF16)
    yield
    excl = jnp.dot(stacked, tri, preferred_element_type=F32)
    yield
    excl = [excl[j * blk:(j + 1) * blk] for j in range(n_blocks)]
    run = jnp.zeros((blk, 1), F32)
    ws = [None] * n_blocks
    for j in reversed(range(n_blocks)):
        ws[j] = jnp.exp(lbs[j] + excl[j] + run).astype(BF16)
        run = run + excl[j][:, :1] + l1s[j][:, :1]
    w = jnp.concatenate(ws, axis=1)
    yield
    out = jnp.dot(w, v_ref[0:n, :], preferred_element_type=F32)
    o_ref[a * blk:(a + 1) * blk, :] = out.astype(BF16)


def _sample_pages(q, bias, k_refs, v_refs, run_scr, acc_scr):
    page_cols = k_refs[0].shape[0]
    n_chunks = page_cols // LANE
    head = lax.broadcasted_iota(jnp.int32, (N_HEADS, page_cols), 0)
    col = lax.broadcasted_iota(jnp.int32, (N_HEADS, page_cols), 1)
    own = (col % N_HEADS) == head
    tri_ones = jnp.concatenate([_strict_lower(LANE), jnp.ones((LANE, LANE), BF16)], axis=1)
    scores = [lax.dot_general(q, k[...].astype(BF16), (((1,), (1,)), ((), ())),
                              preferred_element_type=F32) for k in k_refs]
    yield
    log_betas, stacked = [], []
    for s in scores:
        log_beta, log_1m = _log_terms(s * (1.0 / math.sqrt(HEAD_DIM)) + bias)
        log_1m = jnp.where(own, log_1m, 0.0)
        log_betas.append(log_beta)
        stacked += [log_1m[:, c * LANE:(c + 1) * LANE] for c in range(n_chunks)]
    stacked = jnp.concatenate(stacked, axis=0).astype(BF16)
    yield
    sums = jnp.dot(stacked, tri_ones, preferred_element_type=F32)
    yield
    run = run_scr[...]
    weights = []
    for r in range(len(k_refs)):
        later = [None] * n_chunks
        for c in reversed(range(n_chunks)):
            rows = sums[(r * n_chunks + c) * N_HEADS:(r * n_chunks + c + 1) * N_HEADS]
            later[c] = rows[:, :LANE] + run
            run = run + rows[:, LANE:]
        w = jnp.exp(log_betas[r] + jnp.concatenate(later, axis=1))
        weights.append(jnp.where(own, w, 0.0).astype(BF16))
    run_scr[...] = run
    yield
    acc = acc_scr[...]
    for w, v in zip(weights, v_refs):
        acc = acc + jnp.dot(w, v[...].astype(BF16), preferred_element_type=F32)
    acc_scr[...] = acc


def _interleave(generators, in_flight):
    waiting, active = list(generators), []
    while waiting or active:
        active = [g for g in active if next(g, "done") != "done"]
        if waiting and len(active) < in_flight:
            g = waiting.pop(0)
            next(g)
            active.append(g)


def _attn_kernel(pt_ref, bias_ref, q_ref, k_ref, v_ref, qs_ref, bias_col_ref, ck_ref, cv_ref,
                 o_ref, os_ref, k_buf, v_buf, sem, run_scr, acc_scr,
                 *, blk, groups, steps_per_seq, first_page, in_flight):
    pages_per_step = k_buf.shape[1]
    n_pages = pages_per_step * steps_per_seq
    group = pl.program_id(2)
    step = (pl.program_id(0) * pl.num_programs(1) + pl.program_id(1)) * groups + group
    n_steps = pl.num_programs(0) * pl.num_programs(1) * groups
    within = lax.rem(step, steps_per_seq)
    slot = lax.rem(step, 2)
    n_blocks = q_ref.shape[0] // blk

    def page_copies(of_step, to_slot):
        seq_id = lax.div(of_step, steps_per_seq)
        part = lax.rem(of_step, steps_per_seq)
        copies = []
        for r in range(pages_per_step):
            page = first_page + pt_ref[seq_id, n_pages - 1 - (part * pages_per_step + r)]
            copies.append((pltpu.make_async_copy(ck_ref.at[page], k_buf.at[to_slot, r], sem.at[to_slot, 0]),
                           pltpu.make_async_copy(cv_ref.at[page], v_buf.at[to_slot, r], sem.at[to_slot, 1])))
        return copies

    def start(copies):
        for k_copy, v_copy in copies:
            k_copy.start(priority=0)
            v_copy.start(priority=1)

    @pl.when(step == 0)
    def _():
        start(page_copies(step, slot))

    @pl.when(step + 1 < n_steps)
    def _():
        start(page_copies(step + 1, 1 - slot))

    for k_copy, v_copy in page_copies(step, slot):
        k_copy.wait()
        v_copy.wait()
    k_pages = [k_buf.at[slot, r] for r in range(pages_per_step)]
    v_pages = [v_buf.at[slot, r] for r in range(pages_per_step)]

    @pl.when(within == 0)
    def _():
        run_scr[...] = jnp.zeros_like(run_scr)
        acc_scr[...] = jnp.zeros_like(acc_scr)

    bias = bias_ref[pl.program_id(1)]
    tri = _strict_lower(blk)
    row = lax.broadcasted_iota(jnp.int32, (blk, blk), 0)
    col = lax.broadcasted_iota(jnp.int32, (blk, blk), 1)
    causal = col < row

    def body(p):
        mine = [a for a in reversed(range(n_blocks)) if a % (2 * groups) in (p, 2 * groups - 1 - p)]
        _interleave(
            [_sample_pages(qs_ref[0], bias_col_ref[...], k_pages, v_pages, run_scr, acc_scr)]
            + [_attn_rows(q_ref, k_ref, v_ref, o_ref, bias, tri, causal, a, blk) for a in mine],
            in_flight)

    for p in range(groups):
        pl.when(group == p)(functools.partial(body, p))

    @pl.when(within == steps_per_seq - 1)
    def _():
        os_ref[0] = acc_scr[...].astype(BF16)


def _attn(proj, sb_bias, batch, seq, q_s, cache_k, cache_v, layer, page_table, blk, groups, in_flight):
    db = q_s.shape[0]
    depth, n_pool, page, heads, d = cache_k.shape
    n_pages = page_table.shape[1]
    steps = batch * heads * groups
    assert (seq // blk) % (2 * groups) == 0 and (db * n_pages) % steps == 0
    pages_per_step = db * n_pages // steps
    assert n_pages % pages_per_step == 0
    steps_per_seq = n_pages // pages_per_step
    ck = cache_k.reshape(depth * n_pool, page * heads, d)
    cv = cache_v.reshape(depth * n_pool, page * heads, d)
    q_col, k_col, v_col = (2 * A_WIDTH) // LANE, (2 * A_WIDTH + SB_WIDTH) // LANE, (2 * A_WIDTH + 2 * SB_WIDTH) // LANE

    def head_spec(col0):
        return pl.BlockSpec((seq, HEAD_DIM), lambda b, h, p, pt, bias: (b, col0 + h))

    def seq_spec():
        return pl.BlockSpec((1, heads, d), lambda b, h, p, pt, bias:
                            (lax.div((b * heads + h) * groups + p, steps_per_seq), 0, 0))

    page_buf = pltpu.VMEM((2, pages_per_step, page * heads, d), F32)
    grid_spec = pltpu.PrefetchScalarGridSpec(
        num_scalar_prefetch=2,
        grid=(batch, heads, groups),
        in_specs=[head_spec(q_col), head_spec(k_col), head_spec(v_col), seq_spec(),
                  pl.BlockSpec((heads, 1), lambda b, h, p, pt, bias: (0, 0)),
                  pl.BlockSpec(memory_space=pl.ANY), pl.BlockSpec(memory_space=pl.ANY)],
        out_specs=[head_spec(0), seq_spec()],
        scratch_shapes=[page_buf, page_buf, pltpu.SemaphoreType.DMA((2, 2)),
                        pltpu.VMEM((heads, LANE), F32), pltpu.VMEM((heads, d), F32)],
    )
    return pl.pallas_call(
        functools.partial(_attn_kernel, blk=blk, groups=groups, steps_per_seq=steps_per_seq,
                          first_page=layer * n_pool, in_flight=in_flight),
        grid_spec=grid_spec,
        out_shape=[jax.ShapeDtypeStruct((batch * seq, SB_WIDTH), BF16),
                   jax.ShapeDtypeStruct((db, heads, d), BF16)],
        compiler_params=_params(("arbitrary", "arbitrary", "arbitrary")),
    )(page_table, sb_bias, proj, proj, proj, q_s, sb_bias.reshape(heads, 1), ck, cv)


def _mix_kernel(u_ref, va_ref, b_ref, ga0_ref, ga1_ref, gb0_ref, gb1_ref, x_ref, gt_ref, gpost_ref,
                ws_ref, bs_ref, wa_ref, wb_ref, wo_ref, *rest, chunked):
    n_side = len(rest) // 2
    o_ref = rest[n_side]
    for src, dst in zip(rest[:n_side], rest[n_side + 1:]):
        dst[...] = src[...].astype(BF16)
    bm = u_ref.shape[0]
    u = u_ref[...].astype(F32)
    if chunked:
        row = lax.broadcasted_iota(jnp.int32, (CHUNK, CHUNK), 0)
        col = lax.broadcasted_iota(jnp.int32, (CHUNK, CHUNK), 1)
        keep = col <= row
        cols = []
        for g in range(A_GROUPS):
            w = jnp.where(keep, ws_ref[g], 0.0).astype(BF16)
            b = bs_ref[g]
            rows = []
            for c in range(bm // CHUNK):
                v = va_ref[c * CHUNK:(c + 1) * CHUNK, g * CHUNK:(g + 1) * CHUNK]
                rows.append(jnp.dot(w, v, preferred_element_type=F32) + b)
            cols.append(jnp.concatenate(rows, axis=0))
        s = jnp.concatenate(cols, axis=1)
    else:
        s = va_ref[...].astype(F32) * ws_ref[...] + bs_ref[...]
    a = (u * s).astype(BF16)
    p = jnp.dot(a, wa_ref[...], preferred_element_type=F32)
    q = jnp.dot(b_ref[...], wb_ref[...], preferred_element_type=F32)
    half = p.shape[1] // 2
    merged = jnp.concatenate([
        ga0_ref[...].astype(F32) * p[:, :half] + gb0_ref[...].astype(F32) * q[:, :half],
        ga1_ref[...].astype(F32) * p[:, half:] + gb1_ref[...].astype(F32) * q[:, half:],
    ], axis=1).astype(BF16)
    mix = jnp.dot(merged, wo_ref[...], preferred_element_type=F32)
    o_ref[...] = x_ref[...] + gt_ref[0] * _rms(mix, gpost_ref[...])


BF16_ROWS = 16


def _mix(proj, b_out, x, gt, g_post, ws, bs, wa, wb, wo, bm, chunked, round_on_the_side=()):
    m, d = x.shape
    steps = m // bm
    tiles_per_group = steps // gt.shape[0]
    r = gt.shape[1]

    def col_spec(c):
        return pl.BlockSpec((bm, COL_TILE), lambda i: (i, c))

    def whole(a):
        return pl.BlockSpec(a.shape, lambda i: (0,) * a.ndim)

    def slab_spec(a):
        share = 1
        while (a.shape[0] * share) % (steps * BF16_ROWS):
            share *= 2
        assert steps % share == 0
        return pl.BlockSpec((a.shape[0] * share // steps, a.shape[1]), lambda i: (i // share, 0))

    slabs = [slab_spec(a) for a in round_on_the_side]
    out = pl.pallas_call(
        functools.partial(_mix_kernel, chunked=chunked),
        grid=(steps,),
        in_specs=[
            col_spec(0), col_spec(1),
            pl.BlockSpec((bm, SB_WIDTH), lambda i: (i, 0)),
            col_spec(5), col_spec(6), col_spec(7), col_spec(8),
            pl.BlockSpec((bm, d), lambda i: (i, 0)),
            pl.BlockSpec((1, r, d), lambda i: (i // tiles_per_group, 0, 0)),
            pl.BlockSpec((1, d), lambda i: (0, 0)),
            whole(ws), whole(bs), whole(wa), whole(wb), whole(wo),
        ] + slabs,
        out_specs=[pl.BlockSpec((bm, d), lambda i: (i, 0))] + slabs,
        out_shape=[jax.ShapeDtypeStruct((m, d), F32)]
        + [jax.ShapeDtypeStruct(a.shape, BF16) for a in round_on_the_side],
        compiler_params=_params(("arbitrary",)),
    )(proj, proj, b_out, proj, proj, proj, proj, x, gt, g_post.reshape(1, d), ws, bs, wa, wb, wo,
      *round_on_the_side)
    return out[0], out[1:]


def _ffn_kernel(x_ref, sc_ref, sh_ref, gt_ref, gpre_ref, gpost_ref, wg_ref, wu_ref, wo_ref, o_ref,
                h_scr, acc_scr):
    j = pl.program_id(1)
    last = pl.num_programs(1) - 1
    bm = x_ref.shape[0]
    row_blocks = [slice(r, r + min(bm, ROW_SPLIT)) for r in range(0, bm, min(bm, ROW_SPLIT))]

    def rows_of(ref, rs):
        return ref[0] if ref.shape[1] == 1 else ref[0, rs]

    def weights():
        return wg_ref[...], wu_ref[...], wo_ref[...]

    def partial_out(h, w):
        gate = jnp.dot(h, w[0], preferred_element_type=F32)
        up = jnp.dot(h, w[1], preferred_element_type=F32)
        act = (gate * _sigmoid(gate) * up).astype(BF16)
        return jnp.dot(act, w[2], preferred_element_type=F32)

    @pl.when(j == 0)
    def _():
        w = weights()
        for rs in row_blocks:
            h = (_rms(x_ref[rs, :], gpre_ref[...]) * (1.0 + rows_of(sc_ref, rs))
                 + rows_of(sh_ref, rs)).astype(BF16)
            h_scr[rs, :] = h
            acc_scr[rs, :] = partial_out(h, w)

    @pl.when(jnp.logical_and(j > 0, j < last))
    def _():
        acc_scr[...] += partial_out(h_scr[...], weights())

    @pl.when(j == last)
    def _():
        w = weights()
        for rs in row_blocks:
            f = acc_scr[rs, :] + partial_out(h_scr[rs, :], w)
            o_ref[rs, :] = x_ref[rs, :] + rows_of(gt_ref, rs) * _rms(f, gpost_ref[...])


def _ffn(x, sc, sh, gt, g_pre, g_post, w_ffn_in, w_ffn_out, bm, tf):
    m, d = x.shape
    d_ff = w_ffn_out.shape[0]
    nf = d_ff // tf
    assert nf >= 2, "the kernel treats the first and last d_ff steps separately"
    tiles_per_group = m // bm // sc.shape[0]
    r = sc.shape[1]
    mod_spec = pl.BlockSpec((1, r, d), lambda i, j: (i // tiles_per_group, 0, 0))
    vec_spec = pl.BlockSpec((1, d), lambda i, j: (0, 0))
    return pl.pallas_call(
        _ffn_kernel,
        grid=(m // bm, nf),
        in_specs=[
            pl.BlockSpec((bm, d), lambda i, j: (i, 0)),
            mod_spec, mod_spec, mod_spec, vec_spec, vec_spec,
            pl.BlockSpec((d, tf), lambda i, j: (0, j)),
            pl.BlockSpec((d, tf), lambda i, j: (0, nf + j)),
            pl.BlockSpec((tf, d), lambda i, j: (j, 0)),
        ],
        out_specs=pl.BlockSpec((bm, d), lambda i, j: (i, 0)),
        out_shape=jax.ShapeDtypeStruct((m, d), F32),
        scratch_shapes=[pltpu.VMEM((bm, d), BF16), pltpu.VMEM((bm, d), F32)],
        compiler_params=_params(("arbitrary", "arbitrary")),
    )(x, sc, sh, gt, g_pre.reshape(1, d), g_post.reshape(1, d), w_ffn_in, w_ffn_in, w_ffn_out)


def kernel(x_prompt, x_sample, c_prompt, c_sample, cache_k, cache_v, page_table, w_ada, b_ada, g_pre_mix, g_post_mix, w_in, ln_v_g, ln_v_b, w_s, b_s, sb_bias, w_branch_a, w_branch_b, w_out, g_pre_ffn, g_post_ffn, w_ffn_in, w_ffn_out):
    batch, seq, d = x_prompt.shape
    db, dseq, _ = x_sample.shape
    depth = w_ada.shape[0]
    assert dseq == 1, "the sample group decodes one token per sequence"
    mp, ms = batch * seq, db * dseq
    yp = x_prompt.reshape(mp, d)
    ys = x_sample.reshape(ms, d)
    c_all = jnp.concatenate([c_prompt, c_sample], axis=0)
    outs = {name: [] for name in ("kp", "vp", "ks", "vs", "cv")}

    for l in range(depth):
        w_in_l = w_in[l].astype(BF16)
        wa_l = w_branch_a[l].astype(BF16)
        wb_l = w_branch_b[l].astype(BF16)
        wo_l = w_out[l].astype(BF16)

        mods = _ada(c_all, w_ada[l], b_ada[l])
        mods_p = [m.reshape(batch, 1, d) for m in jnp.split(mods[:batch], 6, axis=-1)]
        mods_s = [m.reshape(1, db, d) for m in jnp.split(mods[batch:], 6, axis=-1)]

        sh1, sc1, gt1, sh2, sc2, gt2 = mods_p
        sh1_s, sc1_s, gt1_s, sh2_s, sc2_s, gt2_s = mods_s
        proj, kp, vp = _inproj(yp, sc1, sh1, g_pre_mix[l], w_in_l, ln_v_g[l], ln_v_b[l],
                               bm=1024, want_va=False)
        proj_s, k_s, v_s, cv_s = _inproj(ys, sc1_s, sh1_s, g_pre_mix[l], w_in_l, ln_v_g[l], ln_v_b[l],
                                         bm=ms, want_va=True)
        q_s = proj_s[:, 2 * A_WIDTH:2 * A_WIDTH + SB_WIDTH].reshape(ms, N_HEADS, HEAD_DIM)
        b_out, b_out_s = _attn(proj, sb_bias[l], batch, seq, q_s, cache_k, cache_v, l, page_table,
                               blk=256, groups=2, in_flight=4)

        x1, (wfi_l, wfo_l) = _mix(proj, b_out, yp, gt1, g_post_mix[l], w_s[l],
                                  b_s[l].reshape(A_GROUPS, CHUNK, 1), wa_l, wb_l, wo_l, bm=256,
                                  chunked=True, round_on_the_side=(w_ffn_in[l], w_ffn_out[l]))
        yp = _ffn(x1, sc2, sh2, gt2, g_pre_ffn[l], g_post_ffn[l], wfi_l, wfo_l, bm=1024, tf=256)
        outs["kp"].append(kp.reshape(batch, seq, N_HEADS, HEAD_DIM))
        outs["vp"].append(vp.reshape(batch, seq, N_HEADS, HEAD_DIM))

        ws_row = jnp.repeat(w_s[l][:, 0, 0], CHUNK).reshape(1, A_WIDTH)
        bs_row = jnp.repeat(b_s[l][:, 0], CHUNK).reshape(1, A_WIDTH)
        x1, _ = _mix(proj_s, b_out_s.reshape(ms, SB_WIDTH), ys, gt1_s, g_post_mix[l], ws_row, bs_row,
                     wa_l, wb_l, wo_l, bm=ms, chunked=False)
        ys = _ffn(x1, sc2_s, sh2_s, gt2_s, g_pre_ffn[l], g_post_ffn[l], wfi_l, wfo_l, bm=ms, tf=512)
        outs["ks"].append(k_s.reshape(db, dseq, N_HEADS, HEAD_DIM))
        outs["vs"].append(v_s.reshape(db, dseq, N_HEADS, HEAD_DIM))
        outs["cv"].append(cv_s.reshape(db, dseq, A_WIDTH))

    return (yp.reshape(batch, seq, d), ys.reshape(db, dseq, d),
            jnp.stack(outs["kp"]), jnp.stack(outs["vp"]),
            jnp.stack(outs["ks"]), jnp.stack(outs["vs"]), jnp.stack(outs["cv"]))
```

```python
import functools
import math

import jax
import jax.numpy as jnp
from jax import lax
from jax.experimental import pallas as pl
from jax.experimental.pallas import tpu as pltpu

F32 = jnp.float32
BF16 = jnp.bfloat16

EPS = 1e-6
LANE = 128
HEAD_DIM = 128
N_HEADS = 8
A_GROUPS = 8
CHUNK = 128
A_WIDTH = A_GROUPS * CHUNK
SB_WIDTH = N_HEADS * HEAD_DIM
COL_TILE = 1024
MXU_COLS = 256
ROW_SPLIT = 256
BF16_ROWS = 16
VMEM_LIMIT = 60 * 1024 * 1024


def _params(semantics):
    return pltpu.CompilerParams(dimension_semantics=semantics, vmem_limit_bytes=VMEM_LIMIT)


def _rms(x, g):
    return x * lax.rsqrt(jnp.mean(x * x, axis=-1, keepdims=True) + EPS) * g


def _sigmoid(x):
    return 1.0 / (1.0 + jnp.exp(-x))


def _ada_kernel(c_ref, w_ref, b_ref, o_ref):
    c = c_ref[...]
    a = (c * _sigmoid(c)).astype(BF16)
    o_ref[...] = jnp.dot(a, w_ref[...].astype(BF16), preferred_element_type=F32) + b_ref[...]


def _ada(c_all, w_ada, b_ada):
    rows, d = c_all.shape
    n = w_ada.shape[1]
    tn = 1024
    return pl.pallas_call(
        _ada_kernel,
        grid=(n // tn,),
        in_specs=[
            pl.BlockSpec((rows, d), lambda j: (0, 0)),
            pl.BlockSpec((d, tn), lambda j: (0, j)),
            pl.BlockSpec((1, tn), lambda j: (0, j)),
        ],
        out_specs=pl.BlockSpec((rows, tn), lambda j: (0, j)),
        out_shape=jax.ShapeDtypeStruct((rows, n), F32),
        compiler_params=_params(("arbitrary",)),
    )(c_all, w_ada, b_ada.reshape(1, n))


def _inproj_kernel(x_ref, sc_ref, sh_ref, g_ref, w_ref, lng_ref, lnb_ref, *rest, want_va, n_side):
    side_in, outs, h_scr = rest[:n_side], rest[n_side:-1], rest[-1]
    proj_ref, k_ref, v_ref = outs[:3]
    va_ref = outs[3] if want_va else None
    round_w = w_ref.dtype != BF16
    w16_ref = outs[3 + want_va] if round_w else None
    side_out = outs[3 + want_va + round_w:]
    j = pl.program_id(1)
    bm = x_ref.shape[0]
    col_chunks = [slice(c, c + MXU_COLS) for c in range(0, COL_TILE, MXU_COLS)]

    def weight_chunks():
        if not round_w:
            return [w_ref[:, cs] for cs in col_chunks]
        ws = [w_ref[:, cs].astype(BF16) for cs in col_chunks]
        for cs, w in zip(col_chunks, ws):
            w16_ref[:, cs] = w
        return ws

    def chunk(h, w):
        return jnp.dot(h, w, preferred_element_type=F32)

    @pl.when(j == 0)
    def _():
        for src, dst in zip(side_in, side_out):
            dst[...] = src[...].astype(BF16)
        ws = weight_chunks()
        for r in range(0, bm, min(bm, ROW_SPLIT)):
            rs = slice(r, r + min(bm, ROW_SPLIT))
            sc = sc_ref[0] if sc_ref.shape[1] == 1 else sc_ref[0, rs]
            sh = sh_ref[0] if sh_ref.shape[1] == 1 else sh_ref[0, rs]
            h = (_rms(x_ref[rs, :], g_ref[...]) * (1.0 + sc) + sh).astype(BF16)
            h_scr[rs, :] = h
            for cs, w in zip(col_chunks, ws):
                proj_ref[rs, cs] = jax.nn.gelu(chunk(h, w)).astype(BF16)

    @pl.when(j == 1)
    def _():
        ws = weight_chunks()
        for r in range(0, bm, min(bm, ROW_SPLIT)):
            rs = slice(r, r + min(bm, ROW_SPLIT))
            h = h_scr[rs, :]
            gs = [jax.nn.gelu(chunk(h, w)) for w in ws]
            mean = sum(jnp.sum(g, axis=-1, keepdims=True) for g in gs) * (1.0 / COL_TILE)
            xcs = [g - mean for g in gs]
            var = sum(jnp.sum(xc * xc, axis=-1, keepdims=True) for xc in xcs) * (1.0 / COL_TILE)
            inv = lax.rsqrt(var + EPS)
            for cs, xc in zip(col_chunks, xcs):
                y = xc * inv * lng_ref[:, cs] + lnb_ref[:, cs]
                if va_ref is not None:
                    va_ref[rs, cs] = y
                proj_ref[rs, cs] = y.astype(BF16)

    def raw(f32_ref):
        h = h_scr[...]
        for cs, w in zip(col_chunks, weight_chunks()):
            acc = chunk(h, w)
            if f32_ref is not None:
                f32_ref[:, cs] = acc
            proj_ref[:, cs] = acc.astype(BF16)

    pl.when(j == 2)(lambda: raw(None))
    pl.when(j == 3)(lambda: raw(k_ref))
    pl.when(j == 4)(lambda: raw(v_ref))

    @pl.when(j >= 5)
    def _():
        h = h_scr[...]
        for cs, w in zip(col_chunks, weight_chunks()):
            proj_ref[:, cs] = _sigmoid(chunk(h, w)).astype(BF16)


def _slab_spec(a, steps, index_of):
    share = 1
    while (a.shape[0] * share) % (steps * BF16_ROWS):
        share *= 2
    assert steps % share == 0
    return pl.BlockSpec((a.shape[0] * share // steps, a.shape[1]),
                        lambda *grid: (index_of(*grid) // share, 0))


def _inproj(x, sc, sh, g_pre, w_in, ln_g, ln_b, bm, want_va, round_on_the_side=()):
    m, d = x.shape
    n_in = w_in.shape[1]
    tiles = m // bm
    tiles_per_group = tiles // sc.shape[0]
    r = sc.shape[1]
    mod_spec = pl.BlockSpec((1, r, d), lambda i, j: (i // tiles_per_group, 0, 0))
    row_spec = pl.BlockSpec((bm, COL_TILE), lambda i, j: (i, 0))
    w_spec = pl.BlockSpec((d, COL_TILE), lambda i, j: (0, j))
    n_f32 = 3 if want_va else 2
    round_w = w_in.dtype != BF16
    assert not round_w or tiles == 1, "the weight copy is written once per column tile"
    slabs = [_slab_spec(a, tiles, lambda i, j: i) for a in round_on_the_side]
    out = pl.pallas_call(
        functools.partial(_inproj_kernel, want_va=want_va, n_side=len(slabs)),
        grid=(tiles, n_in // COL_TILE),
        in_specs=[
            pl.BlockSpec((bm, d), lambda i, j: (i, 0)),
            mod_spec, mod_spec,
            pl.BlockSpec((1, d), lambda i, j: (0, 0)),
            w_spec,
            pl.BlockSpec((1, A_WIDTH), lambda i, j: (0, 0)),
            pl.BlockSpec((1, A_WIDTH), lambda i, j: (0, 0)),
        ] + slabs,
        out_specs=[pl.BlockSpec((bm, COL_TILE), lambda i, j: (i, j))] + [row_spec] * n_f32
        + [w_spec] * round_w + slabs,
        out_shape=[jax.ShapeDtypeStruct((m, n_in), BF16)]
        + [jax.ShapeDtypeStruct((m, COL_TILE), F32)] * n_f32
        + [jax.ShapeDtypeStruct(w_in.shape, BF16)] * round_w
        + [jax.ShapeDtypeStruct(a.shape, BF16) for a in round_on_the_side],
        scratch_shapes=[pltpu.VMEM((bm, d), BF16)],
        compiler_params=_params(("arbitrary", "arbitrary")),
    )(x, sc, sh, g_pre.reshape(1, d), w_in, ln_g.reshape(1, A_WIDTH), ln_b.reshape(1, A_WIDTH),
      *round_on_the_side)
    n_main = 1 + n_f32 + round_w
    return (*out[:n_main], tuple(out[n_main:]))


def _log_terms(z):
    soft = jnp.log(1.0 + jnp.exp(-jnp.abs(z)))
    log_beta = jnp.minimum(z, 0.0) - soft
    return log_beta, log_beta - z


def _strict_lower(n):
    row = lax.broadcasted_iota(jnp.int32, (n, n), 0)
    col = lax.broadcasted_iota(jnp.int32, (n, n), 1)
    return (row > col).astype(BF16)


MASKED_LOG = -1e30


def _attn_rows(q_ref, k_ref, v_ref, o_ref, bias, tri, causal, a, blk):
    n_blocks = a + 1
    n = n_blocks * blk
    q = q_ref[a * blk:(a + 1) * blk, :]
    s = lax.dot_general(q, k_ref[0:n, :], (((1,), (1,)), ((), ())), preferred_element_type=F32)
    yield
    z = s * (1.0 / math.sqrt(HEAD_DIM)) + bias
    log_beta, log_1m = _log_terms(z)
    lbs = [log_beta[:, j * blk:(j + 1) * blk] for j in range(n_blocks)]
    l1s = [log_1m[:, j * blk:(j + 1) * blk] for j in range(n_blocks)]
    lbs[-1] = jnp.where(causal, lbs[-1], MASKED_LOG)
    l1s[-1] = jnp.where(causal, l1s[-1], 0.0)
    stacked = jnp.concatenate(l1s, axis=0).astype(BF16)
    yield
    excl = jnp.dot(stacked, tri, preferred_element_type=F32)
    yield
    excl = [excl[j * blk:(j + 1) * blk] for j in range(n_blocks)]
    run = jnp.zeros((blk, 1), F32)
    ws = [None] * n_blocks
    for j in reversed(range(n_blocks)):
        ws[j] = jnp.exp(lbs[j] + excl[j] + run).astype(BF16)
        run = run + excl[j][:, :1] + l1s[j][:, :1]
    w = jnp.concatenate(ws, axis=1)
    yield
    out = jnp.dot(w, v_ref[0:n, :], preferred_element_type=F32)
    o_ref[a * blk:(a + 1) * blk, :] = out.astype(BF16)


def _sample_pages(q, bias, k_refs, v_refs, run_scr, acc_scr):
    page_cols = k_refs[0].shape[0]
    n_chunks = page_cols // LANE
    head = lax.broadcasted_iota(jnp.int32, (N_HEADS, page_cols), 0)
    col = lax.broadcasted_iota(jnp.int32, (N_HEADS, page_cols), 1)
    own = (col % N_HEADS) == head
    tri_ones = jnp.concatenate([_strict_lower(LANE), jnp.ones((LANE, LANE), BF16)], axis=1)
    scores = [lax.dot_general(q, k[...].astype(BF16), (((1,), (1,)), ((), ())),
                              preferred_element_type=F32) for k in k_refs]
    yield
    log_betas, stacked = [], []
    for s in scores:
        log_beta, log_1m = _log_terms(s * (1.0 / math.sqrt(HEAD_DIM)) + bias)
        log_1m = jnp.where(own, log_1m, 0.0)
        log_betas.append(log_beta)
        stacked += [log_1m[:, c * LANE:(c + 1) * LANE] for c in range(n_chunks)]
    stacked = jnp.concatenate(stacked, axis=0).astype(BF16)
    yield
    sums = jnp.dot(stacked, tri_ones, preferred_element_type=F32)
    yield
    run = run_scr[...]
    weights = []
    for r in range(len(k_refs)):
        later = [None] * n_chunks
        for c in reversed(range(n_chunks)):
            rows = sums[(r * n_chunks + c) * N_HEADS:(r * n_chunks + c + 1) * N_HEADS]
            later[c] = rows[:, :LANE] + run
            run = run + rows[:, LANE:]
        w = jnp.exp(log_betas[r] + jnp.concatenate(later, axis=1))
        weights.append(jnp.where(own, w, 0.0).astype(BF16))
    run_scr[...] = run
    yield
    acc = acc_scr[...]
    for w, v in zip(weights, v_refs):
        acc = acc + jnp.dot(w, v[...].astype(BF16), preferred_element_type=F32)
    acc_scr[...] = acc


def _interleave(generators, in_flight):
    waiting, active = list(generators), []
    while waiting or active:
        active = [g for g in active if next(g, "done") != "done"]
        if waiting and len(active) < in_flight:
            g = waiting.pop(0)
            next(g)
            active.append(g)


def _attn_kernel(pt_ref, bias_ref, q_ref, k_ref, v_ref, qs_ref, bias_col_ref, ck_ref, cv_ref,
                 o_ref, os_ref, k_buf, v_buf, sem, run_scr, acc_scr,
                 *, blk, groups, steps_per_seq, first_page, in_flight):
    pages_per_step = k_buf.shape[1]
    n_pages = pages_per_step * steps_per_seq
    group = pl.program_id(2)
    step = (pl.program_id(0) * pl.num_programs(1) + pl.program_id(1)) * groups + group
    n_steps = pl.num_programs(0) * pl.num_programs(1) * groups
    within = lax.rem(step, steps_per_seq)
    slot = lax.rem(step, 2)
    n_blocks = q_ref.shape[0] // blk

    def page_copies(of_step, to_slot):
        seq_id = lax.div(of_step, steps_per_seq)
        part = lax.rem(of_step, steps_per_seq)
        copies = []
        for r in range(pages_per_step):
            page = first_page + pt_ref[seq_id, n_pages - 1 - (part * pages_per_step + r)]
            copies.append((pltpu.make_async_copy(ck_ref.at[page], k_buf.at[to_slot, r], sem.at[to_slot, 0]),
                           pltpu.make_async_copy(cv_ref.at[page], v_buf.at[to_slot, r], sem.at[to_slot, 1])))
        return copies

    def start(copies):
        for k_copy, v_copy in copies:
            k_copy.start(priority=0)
            v_copy.start(priority=1)

    @pl.when(step == 0)
    def _():
        start(page_copies(step, slot))

    @pl.when(step + 1 < n_steps)
    def _():
        start(page_copies(step + 1, 1 - slot))

    for k_copy, v_copy in page_copies(step, slot):
        k_copy.wait()
        v_copy.wait()
    k_pages = [k_buf.at[slot, r] for r in range(pages_per_step)]
    v_pages = [v_buf.at[slot, r] for r in range(pages_per_step)]

    @pl.when(within == 0)
    def _():
        run_scr[...] = jnp.zeros_like(run_scr)
        acc_scr[...] = jnp.zeros_like(acc_scr)

    bias = bias_ref[pl.program_id(1)]
    tri = _strict_lower(blk)
    row = lax.broadcasted_iota(jnp.int32, (blk, blk), 0)
    col = lax.broadcasted_iota(jnp.int32, (blk, blk), 1)
    causal = col < row

    def body(p):
        mine = [a for a in reversed(range(n_blocks)) if a % (2 * groups) in (p, 2 * groups - 1 - p)]
        _interleave(
            [_sample_pages(qs_ref[0], bias_col_ref[...], k_pages, v_pages, run_scr, acc_scr)]
            + [_attn_rows(q_ref, k_ref, v_ref, o_ref, bias, tri, causal, a, blk) for a in mine],
            in_flight)

    for p in range(groups):
        pl.when(group == p)(functools.partial(body, p))

    @pl.when(within == steps_per_seq - 1)
    def _():
        os_ref[0] = acc_scr[...].astype(BF16)


def _attn(proj, sb_bias, batch, seq, q_s, cache_k, cache_v, layer, page_table, blk, groups, in_flight):
    db = q_s.shape[0]
    depth, n_pool, page, heads, d = cache_k.shape
    n_pages = page_table.shape[1]
    steps = batch * heads * groups
    assert (seq // blk) % (2 * groups) == 0 and (db * n_pages) % steps == 0
    pages_per_step = db * n_pages // steps
    assert n_pages % pages_per_step == 0
    steps_per_seq = n_pages // pages_per_step
    ck = cache_k.reshape(depth * n_pool, page * heads, d)
    cv = cache_v.reshape(depth * n_pool, page * heads, d)
    q_col, k_col, v_col = (2 * A_WIDTH) // LANE, (2 * A_WIDTH + SB_WIDTH) // LANE, (2 * A_WIDTH + 2 * SB_WIDTH) // LANE

    def head_spec(col0):
        return pl.BlockSpec((seq, HEAD_DIM), lambda b, h, p, pt, bias: (b, col0 + h))

    def seq_spec():
        return pl.BlockSpec((1, heads, d), lambda b, h, p, pt, bias:
                            (lax.div((b * heads + h) * groups + p, steps_per_seq), 0, 0))

    page_buf = pltpu.VMEM((2, pages_per_step, page * heads, d), F32)
    grid_spec = pltpu.PrefetchScalarGridSpec(
        num_scalar_prefetch=2,
        grid=(batch, heads, groups),
        in_specs=[head_spec(q_col), head_spec(k_col), head_spec(v_col), seq_spec(),
                  pl.BlockSpec((heads, 1), lambda b, h, p, pt, bias: (0, 0)),
                  pl.BlockSpec(memory_space=pl.ANY), pl.BlockSpec(memory_space=pl.ANY)],
        out_specs=[head_spec(0), seq_spec()],
        scratch_shapes=[page_buf, page_buf, pltpu.SemaphoreType.DMA((2, 2)),
                        pltpu.VMEM((heads, LANE), F32), pltpu.VMEM((heads, d), F32)],
    )
    return pl.pallas_call(
        functools.partial(_attn_kernel, blk=blk, groups=groups, steps_per_seq=steps_per_seq,
                          first_page=layer * n_pool, in_flight=in_flight),
        grid_spec=grid_spec,
        out_shape=[jax.ShapeDtypeStruct((batch * seq, SB_WIDTH), BF16),
                   jax.ShapeDtypeStruct((db, heads, d), BF16)],
        compiler_params=_params(("arbitrary", "arbitrary", "arbitrary")),
    )(page_table, sb_bias, proj, proj, proj, q_s, sb_bias.reshape(heads, 1), ck, cv)


def _mix_kernel(u_ref, va_ref, b_ref, ga0_ref, ga1_ref, gb0_ref, gb1_ref, x_ref, gt_ref, gpost_ref,
                ws_ref, bs_ref, wa_ref, wb_ref, wo_ref, *rest, chunked):
    n_side = len(rest) // 2
    o_ref = rest[n_side]
    for src, dst in zip(rest[:n_side], rest[n_side + 1:]):
        dst[...] = src[...].astype(BF16)
    bm = u_ref.shape[0]
    u = u_ref[...].astype(F32)
    if chunked:
        row = lax.broadcasted_iota(jnp.int32, (CHUNK, CHUNK), 0)
        col = lax.broadcasted_iota(jnp.int32, (CHUNK, CHUNK), 1)
        keep = col <= row
        cols = []
        for g in range(A_GROUPS):
            w = jnp.where(keep, ws_ref[g], 0.0).astype(BF16)
            b = bs_ref[g]
            rows = []
            for c in range(bm // CHUNK):
                v = va_ref[c * CHUNK:(c + 1) * CHUNK, g * CHUNK:(g + 1) * CHUNK]
                rows.append(jnp.dot(w, v, preferred_element_type=F32) + b)
            cols.append(jnp.concatenate(rows, axis=0))
        s = jnp.concatenate(cols, axis=1)
    else:
        s = va_ref[...].astype(F32) * ws_ref[...] + bs_ref[...]
    a = (u * s).astype(BF16)
    p = jnp.dot(a, wa_ref[...], preferred_element_type=F32)
    q = jnp.dot(b_ref[...], wb_ref[...], preferred_element_type=F32)
    half = p.shape[1] // 2
    merged = jnp.concatenate([
        ga0_ref[...].astype(F32) * p[:, :half] + gb0_ref[...].astype(F32) * q[:, :half],
        ga1_ref[...].astype(F32) * p[:, half:] + gb1_ref[...].astype(F32) * q[:, half:],
    ], axis=1).astype(BF16)
    mix = jnp.dot(merged, wo_ref[...], preferred_element_type=F32)
    o_ref[...] = x_ref[...] + gt_ref[0] * _rms(mix, gpost_ref[...])


def _mix(proj, b_out, x, gt, g_post, ws, bs, wa, wb, wo, bm, chunked, round_on_the_side=()):
    m, d = x.shape
    steps = m // bm
    tiles_per_group = steps // gt.shape[0]
    r = gt.shape[1]

    def col_spec(c):
        return pl.BlockSpec((bm, COL_TILE), lambda i: (i, c))

    def whole(a):
        return pl.BlockSpec(a.shape, lambda i: (0,) * a.ndim)

    slabs = [_slab_spec(a, steps, lambda i: i) for a in round_on_the_side]
    out = pl.pallas_call(
        functools.partial(_mix_kernel, chunked=chunked),
        grid=(steps,),
        in_specs=[
            col_spec(0), col_spec(1),
            pl.BlockSpec((bm, SB_WIDTH), lambda i: (i, 0)),
            col_spec(5), col_spec(6), col_spec(7), col_spec(8),
            pl.BlockSpec((bm, d), lambda i: (i, 0)),
            pl.BlockSpec((1, r, d), lambda i: (i // tiles_per_group, 0, 0)),
            pl.BlockSpec((1, d), lambda i: (0, 0)),
            whole(ws), whole(bs), whole(wa), whole(wb), whole(wo),
        ] + slabs,
        out_specs=[pl.BlockSpec((bm, d), lambda i: (i, 0))] + slabs,
        out_shape=[jax.ShapeDtypeStruct((m, d), F32)]
        + [jax.ShapeDtypeStruct(a.shape, BF16) for a in round_on_the_side],
        compiler_params=_params(("arbitrary",)),
    )(proj, proj, b_out, proj, proj, proj, proj, x, gt, g_post.reshape(1, d), ws, bs, wa, wb, wo,
      *round_on_the_side)
    return out[0], out[1:]


def _ffn_kernel(x_ref, sc_ref, sh_ref, gt_ref, gpre_ref, gpost_ref, wg_ref, wu_ref, wo_ref, o_ref,
                h_scr, acc_scr):
    j = pl.program_id(1)
    last = pl.num_programs(1) - 1
    bm = x_ref.shape[0]
    row_blocks = [slice(r, r + min(bm, ROW_SPLIT)) for r in range(0, bm, min(bm, ROW_SPLIT))]

    def rows_of(ref, rs):
        return ref[0] if ref.shape[1] == 1 else ref[0, rs]

    def weights():
        return wg_ref[...], wu_ref[...], wo_ref[...]

    def partial_out(h, w):
        gate = jnp.dot(h, w[0], preferred_element_type=F32)
        up = jnp.dot(h, w[1], preferred_element_type=F32)
        act = (gate * _sigmoid(gate) * up).astype(BF16)
        return jnp.dot(act, w[2], preferred_element_type=F32)

    @pl.when(j == 0)
    def _():
        w = weights()
        for rs in row_blocks:
            h = (_rms(x_ref[rs, :], gpre_ref[...]) * (1.0 + rows_of(sc_ref, rs))
                 + rows_of(sh_ref, rs)).astype(BF16)
            h_scr[rs, :] = h
            acc_scr[rs, :] = partial_out(h, w)

    @pl.when(jnp.logical_and(j > 0, j < last))
    def _():
        acc_scr[...] += partial_out(h_scr[...], weights())

    @pl.when(j == last)
    def _():
        w = weights()
        for rs in row_blocks:
            f = acc_scr[rs, :] + partial_out(h_scr[rs, :], w)
            o_ref[rs, :] = x_ref[rs, :] + rows_of(gt_ref, rs) * _rms(f, gpost_ref[...])


def _ffn(x, sc, sh, gt, g_pre, g_post, w_ffn_in, w_ffn_out, bm, tf):
    m, d = x.shape
    d_ff = w_ffn_out.shape[0]
    nf = d_ff // tf
    assert nf >= 2, "the kernel treats the first and last d_ff steps separately"
    tiles_per_group = m // bm // sc.shape[0]
    r = sc.shape[1]
    mod_spec = pl.BlockSpec((1, r, d), lambda i, j: (i // tiles_per_group, 0, 0))
    vec_spec = pl.BlockSpec((1, d), lambda i, j: (0, 0))
    return pl.pallas_call(
        _ffn_kernel,
        grid=(m // bm, nf),
        in_specs=[
            pl.BlockSpec((bm, d), lambda i, j: (i, 0)),
            mod_spec, mod_spec, mod_spec, vec_spec, vec_spec,
            pl.BlockSpec((d, tf), lambda i, j: (0, j)),
            pl.BlockSpec((d, tf), lambda i, j: (0, nf + j)),
            pl.BlockSpec((tf, d), lambda i, j: (j, 0)),
        ],
        out_specs=pl.BlockSpec((bm, d), lambda i, j: (i, 0)),
        out_shape=jax.ShapeDtypeStruct((m, d), F32),
        scratch_shapes=[pltpu.VMEM((bm, d), BF16), pltpu.VMEM((bm, d), F32)],
        compiler_params=_params(("arbitrary", "arbitrary")),
    )(x, sc, sh, gt, g_pre.reshape(1, d), g_post.reshape(1, d), w_ffn_in, w_ffn_in, w_ffn_out)


def kernel(x_prompt, x_sample, c_prompt, c_sample, cache_k, cache_v, page_table, w_ada, b_ada, g_pre_mix, g_post_mix, w_in, ln_v_g, ln_v_b, w_s, b_s, sb_bias, w_branch_a, w_branch_b, w_out, g_pre_ffn, g_post_ffn, w_ffn_in, w_ffn_out):
    batch, seq, d = x_prompt.shape
    db, dseq, _ = x_sample.shape
    depth = w_ada.shape[0]
    assert dseq == 1, "the sample group decodes one token per sequence"
    mp, ms = batch * seq, db * dseq
    yp = x_prompt.reshape(mp, d)
    ys = x_sample.reshape(ms, d)
    c_all = jnp.concatenate([c_prompt, c_sample], axis=0)
    outs = {name: [] for name in ("kp", "vp", "ks", "vs", "cv")}

    for l in range(depth):
        mods = _ada(c_all, w_ada[l], b_ada[l])
        mods_p = [m.reshape(batch, 1, d) for m in jnp.split(mods[:batch], 6, axis=-1)]
        mods_s = [m.reshape(1, db, d) for m in jnp.split(mods[batch:], 6, axis=-1)]

        sh1, sc1, gt1, sh2, sc2, gt2 = mods_p
        sh1_s, sc1_s, gt1_s, sh2_s, sc2_s, gt2_s = mods_s
        proj_s, k_s, v_s, cv_s, w_in_l, _ = _inproj(
            ys, sc1_s, sh1_s, g_pre_mix[l], w_in[l], ln_v_g[l], ln_v_b[l], bm=ms, want_va=True)
        proj, kp, vp, (wa_l, wb_l, wo_l) = _inproj(
            yp, sc1, sh1, g_pre_mix[l], w_in_l, ln_v_g[l], ln_v_b[l], bm=1024, want_va=False,
            round_on_the_side=(w_branch_a[l], w_branch_b[l], w_out[l]))
        q_s = proj_s[:, 2 * A_WIDTH:2 * A_WIDTH + SB_WIDTH].reshape(ms, N_HEADS, HEAD_DIM)
        b_out, b_out_s = _attn(proj, sb_bias[l], batch, seq, q_s, cache_k, cache_v, l, page_table,
                               blk=256, groups=2, in_flight=4)

        x1, (wfi_l, wfo_l) = _mix(proj, b_out, yp, gt1, g_post_mix[l], w_s[l],
                                  b_s[l].reshape(A_GROUPS, CHUNK, 1), wa_l, wb_l, wo_l, bm=256,
                                  chunked=True, round_on_the_side=(w_ffn_in[l], w_ffn_out[l]))
        yp = _ffn(x1, sc2, sh2, gt2, g_pre_ffn[l], g_post_ffn[l], wfi_l, wfo_l, bm=1024, tf=256)
        outs["kp"].append(kp.reshape(batch, seq, N_HEADS, HEAD_DIM))
        outs["vp"].append(vp.reshape(batch, seq, N_HEADS, HEAD_DIM))

        ws_row = jnp.repeat(w_s[l][:, 0, 0], CHUNK).reshape(1, A_WIDTH)
        bs_row = jnp.repeat(b_s[l][:, 0], CHUNK).reshape(1, A_WIDTH)
        x1, _ = _mix(proj_s, b_out_s.reshape(ms, SB_WIDTH), ys, gt1_s, g_post_mix[l], ws_row, bs_row,
                     wa_l, wb_l, wo_l, bm=ms, chunked=False)
        ys = _ffn(x1, sc2_s, sh2_s, gt2_s, g_pre_ffn[l], g_post_ffn[l], wfi_l, wfo_l, bm=ms, tf=512)
        outs["ks"].append(k_s.reshape(db, dseq, N_HEADS, HEAD_DIM))
        outs["vs"].append(v_s.reshape(db, dseq, N_HEADS, HEAD_DIM))
        outs["cv"].append(cv_s.reshape(db, dseq, A_WIDTH))

    return (yp.reshape(batch, seq, d), ys.reshape(db, dseq, d),
            jnp.stack(outs["kp"]), jnp.stack(outs["vp"]),
            jnp.stack(outs["ks"]), jnp.stack(outs["vs"]), jnp.stack(outs["cv"]))
```

```python
import functools
import math

import jax
import jax.numpy as jnp
from jax import lax
from jax.experimental import pallas as pl
from jax.experimental.pallas import tpu as pltpu

F32 = jnp.float32
BF16 = jnp.bfloat16

EPS = 1e-6
LANE = 128
HEAD_DIM = 128
N_HEADS = 8
A_GROUPS = 8
CHUNK = 128
A_WIDTH = A_GROUPS * CHUNK
SB_WIDTH = N_HEADS * HEAD_DIM
COL_TILE = 1024
MXU_COLS = 256
ROW_SPLIT = 256
BF16_ROWS = 16
VMEM_LIMIT = 60 * 1024 * 1024

INPROJ_ROWS = 1024
MIX_ROWS = 256
FFN_ROWS = 1024
FFN_COLS = 256
FFN_COLS_SAMPLE = 512
ATTN_BLOCK = 256
ATTN_GROUPS = 2
ATTN_IN_FLIGHT = 4


def _params(semantics):
    return pltpu.CompilerParams(dimension_semantics=semantics, vmem_limit_bytes=VMEM_LIMIT)


def _rms(x, g):
    return x * lax.rsqrt(jnp.mean(x * x, axis=-1, keepdims=True) + EPS) * g


def _sigmoid(x):
    return 1.0 / (1.0 + jnp.exp(-x))


def _ada_kernel(c_ref, w_ref, b_ref, o_ref):
    c = c_ref[...]
    a = (c * _sigmoid(c)).astype(BF16)
    o_ref[...] = jnp.dot(a, w_ref[...].astype(BF16), preferred_element_type=F32) + b_ref[...]


def _ada(c_all, w_ada, b_ada):
    rows, d = c_all.shape
    n = w_ada.shape[1]
    tn = 1024
    return pl.pallas_call(
        _ada_kernel,
        grid=(n // tn,),
        in_specs=[
            pl.BlockSpec((rows, d), lambda j: (0, 0)),
            pl.BlockSpec((d, tn), lambda j: (0, j)),
            pl.BlockSpec((1, tn), lambda j: (0, j)),
        ],
        out_specs=pl.BlockSpec((rows, tn), lambda j: (0, j)),
        out_shape=jax.ShapeDtypeStruct((rows, n), F32),
        compiler_params=_params(("arbitrary",)),
    )(c_all, w_ada, b_ada.reshape(1, n))


def _inproj_kernel(x_ref, sc_ref, sh_ref, g_ref, w_ref, lng_ref, lnb_ref, *rest, want_va, n_side):
    side_in, outs, h_scr = rest[:n_side], rest[n_side:-1], rest[-1]
    proj_ref, k_ref, v_ref = outs[:3]
    va_ref = outs[3] if want_va else None
    round_w = w_ref.dtype != BF16
    w16_ref = outs[3 + want_va] if round_w else None
    side_out = outs[3 + want_va + round_w:]
    j = pl.program_id(1)
    bm = x_ref.shape[0]
    col_chunks = [slice(c, c + MXU_COLS) for c in range(0, COL_TILE, MXU_COLS)]

    def weight_chunks():
        if not round_w:
            return [w_ref[:, cs] for cs in col_chunks]
        ws = [w_ref[:, cs].astype(BF16) for cs in col_chunks]
        for cs, w in zip(col_chunks, ws):
            w16_ref[:, cs] = w
        return ws

    def chunk(h, w):
        return jnp.dot(h, w, preferred_element_type=F32)

    @pl.when(j == 0)
    def _():
        for src, dst in zip(side_in, side_out):
            dst[...] = src[...].astype(BF16)
        ws = weight_chunks()
        for r in range(0, bm, min(bm, ROW_SPLIT)):
            rs = slice(r, r + min(bm, ROW_SPLIT))
            sc = sc_ref[0] if sc_ref.shape[1] == 1 else sc_ref[0, rs]
            sh = sh_ref[0] if sh_ref.shape[1] == 1 else sh_ref[0, rs]
            h = (_rms(x_ref[rs, :], g_ref[...]) * (1.0 + sc) + sh).astype(BF16)
            h_scr[rs, :] = h
            for cs, w in zip(col_chunks, ws):
                proj_ref[rs, cs] = jax.nn.gelu(chunk(h, w)).astype(BF16)

    @pl.when(j == 1)
    def _():
        ws = weight_chunks()
        for r in range(0, bm, min(bm, ROW_SPLIT)):
            rs = slice(r, r + min(bm, ROW_SPLIT))
            h = h_scr[rs, :]
            gs = [jax.nn.gelu(chunk(h, w)) for w in ws]
            mean = sum(jnp.sum(g, axis=-1, keepdims=True) for g in gs) * (1.0 / COL_TILE)
            xcs = [g - mean for g in gs]
            var = sum(jnp.sum(xc * xc, axis=-1, keepdims=True) for xc in xcs) * (1.0 / COL_TILE)
            inv = lax.rsqrt(var + EPS)
            for cs, xc in zip(col_chunks, xcs):
                y = xc * inv * lng_ref[:, cs] + lnb_ref[:, cs]
                if va_ref is not None:
                    va_ref[rs, cs] = y
                proj_ref[rs, cs] = y.astype(BF16)

    def raw(f32_ref):
        h = h_scr[...]
        for cs, w in zip(col_chunks, weight_chunks()):
            acc = chunk(h, w)
            if f32_ref is not None:
                f32_ref[:, cs] = acc
            proj_ref[:, cs] = acc.astype(BF16)

    pl.when(j == 2)(lambda: raw(None))
    pl.when(j == 3)(lambda: raw(k_ref))
    pl.when(j == 4)(lambda: raw(v_ref))

    @pl.when(j >= 5)
    def _():
        h = h_scr[...]
        for cs, w in zip(col_chunks, weight_chunks()):
            proj_ref[:, cs] = _sigmoid(chunk(h, w)).astype(BF16)


def _slab_spec(a, steps, index_of):
    share = 1
    while (a.shape[0] * share) % (steps * BF16_ROWS):
        share *= 2
    assert steps % share == 0
    return pl.BlockSpec((a.shape[0] * share // steps, a.shape[1]),
                        lambda *grid: (index_of(*grid) // share, 0))


def _inproj(x, sc, sh, g_pre, w_in, ln_g, ln_b, bm, want_va, round_on_the_side=()):
    m, d = x.shape
    n_in = w_in.shape[1]
    tiles = m // bm
    tiles_per_group = tiles // sc.shape[0]
    r = sc.shape[1]
    mod_spec = pl.BlockSpec((1, r, d), lambda i, j: (i // tiles_per_group, 0, 0))
    row_spec = pl.BlockSpec((bm, COL_TILE), lambda i, j: (i, 0))
    w_spec = pl.BlockSpec((d, COL_TILE), lambda i, j: (0, j))
    n_f32 = 3 if want_va else 2
    round_w = w_in.dtype != BF16
    assert not round_w or tiles == 1, "the weight copy is written once per column tile"
    slabs = [_slab_spec(a, tiles, lambda i, j: i) for a in round_on_the_side]
    out = pl.pallas_call(
        functools.partial(_inproj_kernel, want_va=want_va, n_side=len(slabs)),
        grid=(tiles, n_in // COL_TILE),
        in_specs=[
            pl.BlockSpec((bm, d), lambda i, j: (i, 0)),
            mod_spec, mod_spec,
            pl.BlockSpec((1, d), lambda i, j: (0, 0)),
            w_spec,
            pl.BlockSpec((1, A_WIDTH), lambda i, j: (0, 0)),
            pl.BlockSpec((1, A_WIDTH), lambda i, j: (0, 0)),
        ] + slabs,
        out_specs=[pl.BlockSpec((bm, COL_TILE), lambda i, j: (i, j))] + [row_spec] * n_f32
        + [w_spec] * round_w + slabs,
        out_shape=[jax.ShapeDtypeStruct((m, n_in), BF16)]
        + [jax.ShapeDtypeStruct((m, COL_TILE), F32)] * n_f32
        + [jax.ShapeDtypeStruct(w_in.shape, BF16)] * round_w
        + [jax.ShapeDtypeStruct(a.shape, BF16) for a in round_on_the_side],
        scratch_shapes=[pltpu.VMEM((bm, d), BF16)],
        compiler_params=_params(("arbitrary", "arbitrary")),
    )(x, sc, sh, g_pre.reshape(1, d), w_in, ln_g.reshape(1, A_WIDTH), ln_b.reshape(1, A_WIDTH),
      *round_on_the_side)
    n_main = 1 + n_f32 + round_w
    return (*out[:n_main], tuple(out[n_main:]))


def _log_terms(z):
    soft = jnp.log(1.0 + jnp.exp(-jnp.abs(z)))
    log_beta = jnp.minimum(z, 0.0) - soft
    return log_beta, log_beta - z


def _strict_lower(n):
    row = lax.broadcasted_iota(jnp.int32, (n, n), 0)
    col = lax.broadcasted_iota(jnp.int32, (n, n), 1)
    return (row > col).astype(BF16)


MASKED_LOG = -1e30


def _attn_rows(q_ref, k_ref, v_ref, o_ref, bias, tri, causal, a, blk):
    n_blocks = a + 1
    n = n_blocks * blk
    q = q_ref[a * blk:(a + 1) * blk, :]
    s = lax.dot_general(q, k_ref[0:n, :], (((1,), (1,)), ((), ())), preferred_element_type=F32)
    yield
    z = s * (1.0 / math.sqrt(HEAD_DIM)) + bias
    log_beta, log_1m = _log_terms(z)
    lbs = [log_beta[:, j * blk:(j + 1) * blk] for j in range(n_blocks)]
    l1s = [log_1m[:, j * blk:(j + 1) * blk] for j in range(n_blocks)]
    lbs[-1] = jnp.where(causal, lbs[-1], MASKED_LOG)
    l1s[-1] = jnp.where(causal, l1s[-1], 0.0)
    stacked = jnp.concatenate(l1s, axis=0).astype(BF16)
    yield
    excl = jnp.dot(stacked, tri, preferred_element_type=F32)
    yield
    excl = [excl[j * blk:(j + 1) * blk] for j in range(n_blocks)]
    run = jnp.zeros((blk, 1), F32)
    ws = [None] * n_blocks
    for j in reversed(range(n_blocks)):
        ws[j] = jnp.exp(lbs[j] + excl[j] + run).astype(BF16)
        run = run + excl[j][:, :1] + l1s[j][:, :1]
    w = jnp.concatenate(ws, axis=1)
    yield
    out = jnp.dot(w, v_ref[0:n, :], preferred_element_type=F32)
    o_ref[a * blk:(a + 1) * blk, :] = out.astype(BF16)


def _sample_pages(q, bias, k_refs, v_refs, run_scr, acc_scr):
    page_cols = k_refs[0].shape[0]
    n_chunks = page_cols // LANE
    head = lax.broadcasted_iota(jnp.int32, (N_HEADS, page_cols), 0)
    col = lax.broadcasted_iota(jnp.int32, (N_HEADS, page_cols), 1)
    own = (col % N_HEADS) == head
    tri_ones = jnp.concatenate([_strict_lower(LANE), jnp.ones((LANE, LANE), BF16)], axis=1)
    scores = [lax.dot_general(q, k[...].astype(BF16), (((1,), (1,)), ((), ())),
                              preferred_element_type=F32) for k in k_refs]
    yield
    log_betas, stacked = [], []
    for s in scores:
        log_beta, log_1m = _log_terms(s * (1.0 / math.sqrt(HEAD_DIM)) + bias)
        log_1m = jnp.where(own, log_1m, 0.0)
        log_betas.append(log_beta)
        stacked += [log_1m[:, c * LANE:(c + 1) * LANE] for c in range(n_chunks)]
    stacked = jnp.concatenate(stacked, axis=0).astype(BF16)
    yield
    sums = jnp.dot(stacked, tri_ones, preferred_element_type=F32)
    yield
    run = run_scr[...]
    weights = []
    for r in range(len(k_refs)):
        later = [None] * n_chunks
        for c in reversed(range(n_chunks)):
            rows = sums[(r * n_chunks + c) * N_HEADS:(r * n_chunks + c + 1) * N_HEADS]
            later[c] = rows[:, :LANE] + run
            run = run + rows[:, LANE:]
        w = jnp.exp(log_betas[r] + jnp.concatenate(later, axis=1))
        weights.append(jnp.where(own, w, 0.0).astype(BF16))
    run_scr[...] = run
    yield
    acc = acc_scr[...]
    for w, v in zip(weights, v_refs):
        acc = acc + jnp.dot(w, v[...].astype(BF16), preferred_element_type=F32)
    acc_scr[...] = acc


def _interleave(generators, in_flight):
    waiting, active = list(generators), []
    while waiting or active:
        active = [g for g in active if next(g, "done") != "done"]
        if waiting and len(active) < in_flight:
            g = waiting.pop(0)
            next(g)
            active.append(g)


def _attn_kernel(pt_ref, bias_ref, q_ref, k_ref, v_ref, qs_ref, bias_col_ref, ck_ref, cv_ref,
                 o_ref, os_ref, k_buf, v_buf, sem, run_scr, acc_scr,
                 *, blk, groups, steps_per_seq, first_page, in_flight):
    pages_per_step = k_buf.shape[1]
    n_pages = pages_per_step * steps_per_seq
    group = pl.program_id(2)
    step = (pl.program_id(0) * pl.num_programs(1) + pl.program_id(1)) * groups + group
    n_steps = pl.num_programs(0) * pl.num_programs(1) * groups
    within = lax.rem(step, steps_per_seq)
    slot = lax.rem(step, 2)
    n_blocks = q_ref.shape[0] // blk

    def page_copies(of_step, to_slot):
        seq_id = lax.div(of_step, steps_per_seq)
        part = lax.rem(of_step, steps_per_seq)
        copies = []
        for r in range(pages_per_step):
            page = first_page + pt_ref[seq_id, n_pages - 1 - (part * pages_per_step + r)]
            copies.append((pltpu.make_async_copy(ck_ref.at[page], k_buf.at[to_slot, r], sem.at[to_slot, 0]),
                           pltpu.make_async_copy(cv_ref.at[page], v_buf.at[to_slot, r], sem.at[to_slot, 1])))
        return copies

    def start(copies):
        for k_copy, v_copy in copies:
            k_copy.start(priority=0)
            v_copy.start(priority=1)

    @pl.when(step == 0)
    def _():
        start(page_copies(step, slot))

    @pl.when(step + 1 < n_steps)
    def _():
        start(page_copies(step + 1, 1 - slot))

    for k_copy, v_copy in page_copies(step, slot):
        k_copy.wait()
        v_copy.wait()
    k_pages = [k_buf.at[slot, r] for r in range(pages_per_step)]
    v_pages = [v_buf.at[slot, r] for r in range(pages_per_step)]

    @pl.when(within == 0)
    def _():
        run_scr[...] = jnp.zeros_like(run_scr)
        acc_scr[...] = jnp.zeros_like(acc_scr)

    bias = bias_ref[pl.program_id(1)]
    tri = _strict_lower(blk)
    row = lax.broadcasted_iota(jnp.int32, (blk, blk), 0)
    col = lax.broadcasted_iota(jnp.int32, (blk, blk), 1)
    causal = col < row

    def body(p):
        mine = [a for a in reversed(range(n_blocks)) if a % (2 * groups) in (p, 2 * groups - 1 - p)]
        _interleave(
            [_sample_pages(qs_ref[0], bias_col_ref[...], k_pages, v_pages, run_scr, acc_scr)]
            + [_attn_rows(q_ref, k_ref, v_ref, o_ref, bias, tri, causal, a, blk) for a in mine],
            in_flight)

    for p in range(groups):
        pl.when(group == p)(functools.partial(body, p))

    @pl.when(within == steps_per_seq - 1)
    def _():
        os_ref[0] = acc_scr[...].astype(BF16)


def _attn(proj, sb_bias, batch, seq, q_s, cache_k, cache_v, layer, page_table, blk, groups, in_flight):
    db = q_s.shape[0]
    depth, n_pool, page, heads, d = cache_k.shape
    n_pages = page_table.shape[1]
    steps = batch * heads * groups
    assert (seq // blk) % (2 * groups) == 0 and (db * n_pages) % steps == 0
    pages_per_step = db * n_pages // steps
    assert n_pages % pages_per_step == 0
    steps_per_seq = n_pages // pages_per_step
    ck = cache_k.reshape(depth * n_pool, page * heads, d)
    cv = cache_v.reshape(depth * n_pool, page * heads, d)
    q_col, k_col, v_col = (2 * A_WIDTH) // LANE, (2 * A_WIDTH + SB_WIDTH) // LANE, (2 * A_WIDTH + 2 * SB_WIDTH) // LANE

    def head_spec(col0):
        return pl.BlockSpec((seq, HEAD_DIM), lambda b, h, p, pt, bias: (b, col0 + h))

    def seq_spec():
        return pl.BlockSpec((1, heads, d), lambda b, h, p, pt, bias:
                            (lax.div((b * heads + h) * groups + p, steps_per_seq), 0, 0))

    page_buf = pltpu.VMEM((2, pages_per_step, page * heads, d), F32)
    grid_spec = pltpu.PrefetchScalarGridSpec(
        num_scalar_prefetch=2,
        grid=(batch, heads, groups),
        in_specs=[head_spec(q_col), head_spec(k_col), head_spec(v_col), seq_spec(),
                  pl.BlockSpec((heads, 1), lambda b, h, p, pt, bias: (0, 0)),
                  pl.BlockSpec(memory_space=pl.ANY), pl.BlockSpec(memory_space=pl.ANY)],
        out_specs=[head_spec(0), seq_spec()],
        scratch_shapes=[page_buf, page_buf, pltpu.SemaphoreType.DMA((2, 2)),
                        pltpu.VMEM((heads, LANE), F32), pltpu.VMEM((heads, d), F32)],
    )
    return pl.pallas_call(
        functools.partial(_attn_kernel, blk=blk, groups=groups, steps_per_seq=steps_per_seq,
                          first_page=layer * n_pool, in_flight=in_flight),
        grid_spec=grid_spec,
        out_shape=[jax.ShapeDtypeStruct((batch * seq, SB_WIDTH), BF16),
                   jax.ShapeDtypeStruct((db, heads, d), BF16)],
        compiler_params=_params(("arbitrary", "arbitrary", "arbitrary")),
    )(page_table, sb_bias, proj, proj, proj, q_s, sb_bias.reshape(heads, 1), ck, cv)


def _mix_kernel(u_ref, va_ref, b_ref, ga0_ref, ga1_ref, gb0_ref, gb1_ref, x_ref, gt_ref, gpost_ref,
                ws_ref, bs_ref, wa_ref, wb_ref, wo_ref, *rest, chunked):
    n_side = len(rest) // 2
    o_ref = rest[n_side]
    for src, dst in zip(rest[:n_side], rest[n_side + 1:]):
        dst[...] = src[...].astype(BF16)
    bm = u_ref.shape[0]
    u = u_ref[...].astype(F32)
    if chunked:
        row = lax.broadcasted_iota(jnp.int32, (CHUNK, CHUNK), 0)
        col = lax.broadcasted_iota(jnp.int32, (CHUNK, CHUNK), 1)
        keep = col <= row
        cols = []
        for g in range(A_GROUPS):
            w = jnp.where(keep, ws_ref[g], 0.0).astype(BF16)
            b = bs_ref[g]
            rows = []
            for c in range(bm // CHUNK):
                v = va_ref[c * CHUNK:(c + 1) * CHUNK, g * CHUNK:(g + 1) * CHUNK]
                rows.append(jnp.dot(w, v, preferred_element_type=F32) + b)
            cols.append(jnp.concatenate(rows, axis=0))
        s = jnp.concatenate(cols, axis=1)
    else:
        s = va_ref[...].astype(F32) * ws_ref[...] + bs_ref[...]
    a = (u * s).astype(BF16)
    p = jnp.dot(a, wa_ref[...], preferred_element_type=F32)
    q = jnp.dot(b_ref[...], wb_ref[...], preferred_element_type=F32)
    half = p.shape[1] // 2
    merged = jnp.concatenate([
        ga0_ref[...].astype(F32) * p[:, :half] + gb0_ref[...].astype(F32) * q[:, :half],
        ga1_ref[...].astype(F32) * p[:, half:] + gb1_ref[...].astype(F32) * q[:, half:],
    ], axis=1).astype(BF16)
    mix = jnp.dot(merged, wo_ref[...], preferred_element_type=F32)
    o_ref[...] = x_ref[...] + gt_ref[0] * _rms(mix, gpost_ref[...])


def _mix(proj, b_out, x, gt, g_post, ws, bs, wa, wb, wo, bm, chunked, round_on_the_side=()):
    m, d = x.shape
    steps = m // bm
    tiles_per_group = steps // gt.shape[0]
    r = gt.shape[1]

    def col_spec(c):
        return pl.BlockSpec((bm, COL_TILE), lambda i: (i, c))

    def whole(a):
        return pl.BlockSpec(a.shape, lambda i: (0,) * a.ndim)

    slabs = [_slab_spec(a, steps, lambda i: i) for a in round_on_the_side]
    out = pl.pallas_call(
        functools.partial(_mix_kernel, chunked=chunked),
        grid=(steps,),
        in_specs=[
            col_spec(0), col_spec(1),
            pl.BlockSpec((bm, SB_WIDTH), lambda i: (i, 0)),
            col_spec(5), col_spec(6), col_spec(7), col_spec(8),
            pl.BlockSpec((bm, d), lambda i: (i, 0)),
            pl.BlockSpec((1, r, d), lambda i: (i // tiles_per_group, 0, 0)),
            pl.BlockSpec((1, d), lambda i: (0, 0)),
            whole(ws), whole(bs), whole(wa), whole(wb), whole(wo),
        ] + slabs,
        out_specs=[pl.BlockSpec((bm, d), lambda i: (i, 0))] + slabs,
        out_shape=[jax.ShapeDtypeStruct((m, d), F32)]
        + [jax.ShapeDtypeStruct(a.shape, BF16) for a in round_on_the_side],
        compiler_params=_params(("arbitrary",)),
    )(proj, proj, b_out, proj, proj, proj, proj, x, gt, g_post.reshape(1, d), ws, bs, wa, wb, wo,
      *round_on_the_side)
    return out[0], out[1:]


def _ffn_kernel(x_ref, sc_ref, sh_ref, gt_ref, gpre_ref, gpost_ref, wg_ref, wu_ref, wo_ref, o_ref,
                h_scr, acc_scr):
    j = pl.program_id(1)
    last = pl.num_programs(1) - 1
    bm = x_ref.shape[0]
    row_blocks = [slice(r, r + min(bm, ROW_SPLIT)) for r in range(0, bm, min(bm, ROW_SPLIT))]

    def rows_of(ref, rs):
        return ref[0] if ref.shape[1] == 1 else ref[0, rs]

    def weights():
        return wg_ref[...], wu_ref[...], wo_ref[...]

    def partial_out(h, w):
        gate = jnp.dot(h, w[0], preferred_element_type=F32)
        up = jnp.dot(h, w[1], preferred_element_type=F32)
        act = (gate * _sigmoid(gate) * up).astype(BF16)
        return jnp.dot(act, w[2], preferred_element_type=F32)

    @pl.when(j == 0)
    def _():
        w = weights()
        for rs in row_blocks:
            h = (_rms(x_ref[rs, :], gpre_ref[...]) * (1.0 + rows_of(sc_ref, rs))
                 + rows_of(sh_ref, rs)).astype(BF16)
            h_scr[rs, :] = h
            acc_scr[rs, :] = partial_out(h, w)

    @pl.when(jnp.logical_and(j > 0, j < last))
    def _():
        acc_scr[...] += partial_out(h_scr[...], weights())

    @pl.when(j == last)
    def _():
        w = weights()
        for rs in row_blocks:
            f = acc_scr[rs, :] + partial_out(h_scr[rs, :], w)
            o_ref[rs, :] = x_ref[rs, :] + rows_of(gt_ref, rs) * _rms(f, gpost_ref[...])


def _ffn(x, sc, sh, gt, g_pre, g_post, w_ffn_in, w_ffn_out, bm, tf):
    m, d = x.shape
    d_ff = w_ffn_out.shape[0]
    nf = d_ff // tf
    assert nf >= 2, "the kernel treats the first and last d_ff steps separately"
    tiles_per_group = m // bm // sc.shape[0]
    r = sc.shape[1]
    mod_spec = pl.BlockSpec((1, r, d), lambda i, j: (i // tiles_per_group, 0, 0))
    vec_spec = pl.BlockSpec((1, d), lambda i, j: (0, 0))
    return pl.pallas_call(
        _ffn_kernel,
        grid=(m // bm, nf),
        in_specs=[
            pl.BlockSpec((bm, d), lambda i, j: (i, 0)),
            mod_spec, mod_spec, mod_spec, vec_spec, vec_spec,
            pl.BlockSpec((d, tf), lambda i, j: (0, j)),
            pl.BlockSpec((d, tf), lambda i, j: (0, nf + j)),
            pl.BlockSpec((tf, d), lambda i, j: (j, 0)),
        ],
        out_specs=pl.BlockSpec((bm, d), lambda i, j: (i, 0)),
        out_shape=jax.ShapeDtypeStruct((m, d), F32),
        scratch_shapes=[pltpu.VMEM((bm, d), BF16), pltpu.VMEM((bm, d), F32)],
        compiler_params=_params(("arbitrary", "arbitrary")),
    )(x, sc, sh, gt, g_pre.reshape(1, d), g_post.reshape(1, d), w_ffn_in, w_ffn_in, w_ffn_out)


def kernel(x_prompt, x_sample, c_prompt, c_sample, cache_k, cache_v, page_table, w_ada, b_ada, g_pre_mix, g_post_mix, w_in, ln_v_g, ln_v_b, w_s, b_s, sb_bias, w_branch_a, w_branch_b, w_out, g_pre_ffn, g_post_ffn, w_ffn_in, w_ffn_out):
    batch, seq, d = x_prompt.shape
    db, dseq, _ = x_sample.shape
    depth = w_ada.shape[0]
    assert dseq == 1, "the sample group decodes one token per sequence"
    mp, ms = batch * seq, db * dseq
    yp = x_prompt.reshape(mp, d)
    ys = x_sample.reshape(ms, d)
    c_all = jnp.concatenate([c_prompt, c_sample], axis=0)
    outs = {name: [] for name in ("kp", "vp", "ks", "vs", "cv")}

    for l in range(depth):
        mods = _ada(c_all, w_ada[l], b_ada[l])
        mods_p = [m.reshape(batch, 1, d) for m in jnp.split(mods[:batch], 6, axis=-1)]
        mods_s = [m.reshape(1, db, d) for m in jnp.split(mods[batch:], 6, axis=-1)]

        sh1, sc1, gt1, sh2, sc2, gt2 = mods_p
        sh1_s, sc1_s, gt1_s, sh2_s, sc2_s, gt2_s = mods_s
        proj_s, k_s, v_s, cv_s, w_in_l, _ = _inproj(
            ys, sc1_s, sh1_s, g_pre_mix[l], w_in[l], ln_v_g[l], ln_v_b[l], bm=ms, want_va=True)
        proj, kp, vp, (wa_l, wb_l, wo_l) = _inproj(
            yp, sc1, sh1, g_pre_mix[l], w_in_l, ln_v_g[l], ln_v_b[l], bm=INPROJ_ROWS, want_va=False,
            round_on_the_side=(w_branch_a[l], w_branch_b[l], w_out[l]))
        q_s = proj_s[:, 2 * A_WIDTH:2 * A_WIDTH + SB_WIDTH].reshape(ms, N_HEADS, HEAD_DIM)
        b_out, b_out_s = _attn(proj, sb_bias[l], batch, seq, q_s, cache_k, cache_v, l, page_table,
                               blk=ATTN_BLOCK, groups=ATTN_GROUPS, in_flight=ATTN_IN_FLIGHT)

        x1, (wfi_l, wfo_l) = _mix(proj, b_out, yp, gt1, g_post_mix[l], w_s[l],
                                  b_s[l].reshape(A_GROUPS, CHUNK, 1), wa_l, wb_l, wo_l, bm=MIX_ROWS,
                                  chunked=True, round_on_the_side=(w_ffn_in[l], w_ffn_out[l]))
        yp = _ffn(x1, sc2, sh2, gt2, g_pre_ffn[l], g_post_ffn[l], wfi_l, wfo_l,
                  bm=FFN_ROWS, tf=FFN_COLS)
        outs["kp"].append(kp.reshape(batch, seq, N_HEADS, HEAD_DIM))
        outs["vp"].append(vp.reshape(batch, seq, N_HEADS, HEAD_DIM))

        ws_row = jnp.repeat(w_s[l][:, 0, 0], CHUNK).reshape(1, A_WIDTH)
        bs_row = jnp.repeat(b_s[l][:, 0], CHUNK).reshape(1, A_WIDTH)
        x1, _ = _mix(proj_s, b_out_s.reshape(ms, SB_WIDTH), ys, gt1_s, g_post_mix[l], ws_row, bs_row,
                     wa_l, wb_l, wo_l, bm=ms, chunked=False)
        ys = _ffn(x1, sc2_s, sh2_s, gt2_s, g_pre_ffn[l], g_post_ffn[l], wfi_l, wfo_l,
                  bm=ms, tf=FFN_COLS_SAMPLE)
        outs["ks"].append(k_s.reshape(db, dseq, N_HEADS, HEAD_DIM))
        outs["vs"].append(v_s.reshape(db, dseq, N_HEADS, HEAD_DIM))
        outs["cv"].append(cv_s.reshape(db, dseq, A_WIDTH))

    return (yp.reshape(batch, seq, d), ys.reshape(db, dseq, d),
            jnp.stack(outs["kp"]), jnp.stack(outs["vp"]),
            jnp.stack(outs["ks"]), jnp.stack(outs["vs"]), jnp.stack(outs["cv"]))
```

```python
import functools
import math

import jax
import jax.numpy as jnp
from jax import lax
from jax.experimental import pallas as pl
from jax.experimental.pallas import tpu as pltpu

F32 = jnp.float32
BF16 = jnp.bfloat16

EPS = 1e-6
LANE = 128
HEAD_DIM = 128
N_HEADS = 8
A_GROUPS = 8
CHUNK = 128
A_WIDTH = A_GROUPS * CHUNK
SB_WIDTH = N_HEADS * HEAD_DIM
COL_TILE = 1024
MXU_COLS = 256
ROW_SPLIT = 256
BF16_ROWS = 16
VMEM_LIMIT = 60 * 1024 * 1024

INPROJ_ROWS = 1024
MIX_ROWS = 256
FFN_ROWS = 1024
FFN_COLS = 256
FFN_COLS_SAMPLE = 512
ATTN_BLOCK = 256
ATTN_GROUPS = 2
ATTN_IN_FLIGHT = 4


def _params(semantics):
    return pltpu.CompilerParams(dimension_semantics=semantics, vmem_limit_bytes=VMEM_LIMIT)


def _rms(x, g):
    return x * lax.rsqrt(jnp.mean(x * x, axis=-1, keepdims=True) + EPS) * g


def _sigmoid(x):
    return 1.0 / (1.0 + jnp.exp(-x))


def _ada_kernel(c_ref, w_ref, b_ref, o_ref):
    c = c_ref[...]
    a = (c * _sigmoid(c)).astype(BF16)
    o_ref[...] = jnp.dot(a, w_ref[...].astype(BF16), preferred_element_type=F32) + b_ref[...]


def _ada(c_all, w_ada, b_ada):
    rows, d = c_all.shape
    n = w_ada.shape[1]
    tn = 1024
    return pl.pallas_call(
        _ada_kernel,
        grid=(n // tn,),
        in_specs=[
            pl.BlockSpec((rows, d), lambda j: (0, 0)),
            pl.BlockSpec((d, tn), lambda j: (0, j)),
            pl.BlockSpec((1, tn), lambda j: (0, j)),
        ],
        out_specs=pl.BlockSpec((rows, tn), lambda j: (0, j)),
        out_shape=jax.ShapeDtypeStruct((rows, n), F32),
        compiler_params=_params(("arbitrary",)),
    )(c_all, w_ada, b_ada.reshape(1, n))


def _inproj_kernel(x_ref, sc_ref, sh_ref, g_ref, w_ref, lng_ref, lnb_ref, *rest, want_va, n_side):
    side_in, outs, h_scr = rest[:n_side], rest[n_side:-1], rest[-1]
    proj_ref, k_ref, v_ref = outs[:3]
    va_ref = outs[3] if want_va else None
    round_w = w_ref.dtype != BF16
    w16_ref = outs[3 + want_va] if round_w else None
    side_out = outs[3 + want_va + round_w:]
    j = pl.program_id(1)
    bm = x_ref.shape[0]
    col_chunks = [slice(c, c + MXU_COLS) for c in range(0, COL_TILE, MXU_COLS)]

    def weight_chunks():
        if not round_w:
            return [w_ref[:, cs] for cs in col_chunks]
        ws = [w_ref[:, cs].astype(BF16) for cs in col_chunks]
        for cs, w in zip(col_chunks, ws):
            w16_ref[:, cs] = w
        return ws

    def chunk(h, w):
        return jnp.dot(h, w, preferred_element_type=F32)

    @pl.when(j == 0)
    def _():
        for src, dst in zip(side_in, side_out):
            dst[...] = src[...].astype(BF16)
        ws = weight_chunks()
        for r in range(0, bm, min(bm, ROW_SPLIT)):
            rs = slice(r, r + min(bm, ROW_SPLIT))
            sc = sc_ref[0] if sc_ref.shape[1] == 1 else sc_ref[0, rs]
            sh = sh_ref[0] if sh_ref.shape[1] == 1 else sh_ref[0, rs]
            h = (_rms(x_ref[rs, :], g_ref[...]) * (1.0 + sc) + sh).astype(BF16)
            h_scr[rs, :] = h
            for cs, w in zip(col_chunks, ws):
                proj_ref[rs, cs] = jax.nn.gelu(chunk(h, w)).astype(BF16)

    @pl.when(j == 1)
    def _():
        ws = weight_chunks()
        for r in range(0, bm, min(bm, ROW_SPLIT)):
            rs = slice(r, r + min(bm, ROW_SPLIT))
            h = h_scr[rs, :]
            gs = [jax.nn.gelu(chunk(h, w)) for w in ws]
            mean = sum(jnp.sum(g, axis=-1, keepdims=True) for g in gs) * (1.0 / COL_TILE)
            xcs = [g - mean for g in gs]
            var = sum(jnp.sum(xc * xc, axis=-1, keepdims=True) for xc in xcs) * (1.0 / COL_TILE)
            inv = lax.rsqrt(var + EPS)
            for cs, xc in zip(col_chunks, xcs):
                y = xc * inv * lng_ref[:, cs] + lnb_ref[:, cs]
                if va_ref is not None:
                    va_ref[rs, cs] = y
                proj_ref[rs, cs] = y.astype(BF16)

    def raw(f32_ref):
        h = h_scr[...]
        for cs, w in zip(col_chunks, weight_chunks()):
            acc = chunk(h, w)
            if f32_ref is not None:
                f32_ref[:, cs] = acc
            proj_ref[:, cs] = acc.astype(BF16)

    pl.when(j == 2)(lambda: raw(None))
    pl.when(j == 3)(lambda: raw(k_ref))
    pl.when(j == 4)(lambda: raw(v_ref))

    @pl.when(j >= 5)
    def _():
        h = h_scr[...]
        for cs, w in zip(col_chunks, weight_chunks()):
            proj_ref[:, cs] = _sigmoid(chunk(h, w)).astype(BF16)


def _slab_spec(a, steps, index_of):
    share = 1
    while (a.shape[0] * share) % (steps * BF16_ROWS):
        share *= 2
    assert steps % share == 0
    return pl.BlockSpec((a.shape[0] * share // steps, a.shape[1]),
                        lambda *grid: (index_of(*grid) // share, 0))


def _inproj(x, sc, sh, g_pre, w_in, ln_g, ln_b, bm, want_va, round_on_the_side=()):
    m, d = x.shape
    n_in = w_in.shape[1]
    tiles = m // bm
    tiles_per_group = tiles // sc.shape[0]
    r = sc.shape[1]
    mod_spec = pl.BlockSpec((1, r, d), lambda i, j: (i // tiles_per_group, 0, 0))
    row_spec = pl.BlockSpec((bm, COL_TILE), lambda i, j: (i, 0))
    w_spec = pl.BlockSpec((d, COL_TILE), lambda i, j: (0, j))
    n_f32 = 3 if want_va else 2
    round_w = w_in.dtype != BF16
    assert not round_w or tiles == 1, "the weight copy is written once per column tile"
    slabs = [_slab_spec(a, tiles, lambda i, j: i) for a in round_on_the_side]
    out = pl.pallas_call(
        functools.partial(_inproj_kernel, want_va=want_va, n_side=len(slabs)),
        grid=(tiles, n_in // COL_TILE),
        in_specs=[
            pl.BlockSpec((bm, d), lambda i, j: (i, 0)),
            mod_spec, mod_spec,
            pl.BlockSpec((1, d), lambda i, j: (0, 0)),
            w_spec,
            pl.BlockSpec((1, A_WIDTH), lambda i, j: (0, 0)),
            pl.BlockSpec((1, A_WIDTH), lambda i, j: (0, 0)),
        ] + slabs,
        out_specs=[pl.BlockSpec((bm, COL_TILE), lambda i, j: (i, j))] + [row_spec] * n_f32
        + [w_spec] * round_w + slabs,
        out_shape=[jax.ShapeDtypeStruct((m, n_in), BF16)]
        + [jax.ShapeDtypeStruct((m, COL_TILE), F32)] * n_f32
        + [jax.ShapeDtypeStruct(w_in.shape, BF16)] * round_w
        + [jax.ShapeDtypeStruct(a.shape, BF16) for a in round_on_the_side],
        scratch_shapes=[pltpu.VMEM((bm, d), BF16)],
        compiler_params=_params(("arbitrary", "arbitrary")),
    )(x, sc, sh, g_pre.reshape(1, d), w_in, ln_g.reshape(1, A_WIDTH), ln_b.reshape(1, A_WIDTH),
      *round_on_the_side)
    n_main = 1 + n_f32 + round_w
    return (*out[:n_main], tuple(out[n_main:]))


def _log_terms(z):
    soft = jnp.log(1.0 + jnp.exp(-jnp.abs(z)))
    log_beta = jnp.minimum(z, 0.0) - soft
    return log_beta, log_beta - z


def _strict_lower(n):
    row = lax.broadcasted_iota(jnp.int32, (n, n), 0)
    col = lax.broadcasted_iota(jnp.int32, (n, n), 1)
    return (row > col).astype(BF16)


MASKED_LOG = -1e30


def _attn_rows(q_ref, k_ref, v_ref, o_ref, bias, tri, causal, a, blk):
    n_blocks = a + 1
    n = n_blocks * blk
    q = q_ref[a * blk:(a + 1) * blk, :]
    s = lax.dot_general(q, k_ref[0:n, :], (((1,), (1,)), ((), ())), preferred_element_type=F32)
    yield
    z = s * (1.0 / math.sqrt(HEAD_DIM)) + bias
    log_beta, log_1m = _log_terms(z)
    lbs = [log_beta[:, j * blk:(j + 1) * blk] for j in range(n_blocks)]
    l1s = [log_1m[:, j * blk:(j + 1) * blk] for j in range(n_blocks)]
    lbs[-1] = jnp.where(causal, lbs[-1], MASKED_LOG)
    l1s[-1] = jnp.where(causal, l1s[-1], 0.0)
    stacked = jnp.concatenate(l1s, axis=0).astype(BF16)
    yield
    excl = jnp.dot(stacked, tri, preferred_element_type=F32)
    yield
    excl = [excl[j * blk:(j + 1) * blk] for j in range(n_blocks)]
    run = jnp.zeros((blk, 1), F32)
    ws = [None] * n_blocks
    for j in reversed(range(n_blocks)):
        ws[j] = jnp.exp(lbs[j] + excl[j] + run).astype(BF16)
        run = run + excl[j][:, :1] + l1s[j][:, :1]
    w = jnp.concatenate(ws, axis=1)
    yield
    out = jnp.dot(w, v_ref[0:n, :], preferred_element_type=F32)
    o_ref[a * blk:(a + 1) * blk, :] = out.astype(BF16)


def _sample_pages(q, bias, k_refs, v_refs, run_scr, acc_scr):
    page_cols = k_refs[0].shape[0]
    n_chunks = page_cols // LANE
    head = lax.broadcasted_iota(jnp.int32, (N_HEADS, page_cols), 0)
    col = lax.broadcasted_iota(jnp.int32, (N_HEADS, page_cols), 1)
    own = (col % N_HEADS) == head
    tri_ones = jnp.concatenate([_strict_lower(LANE), jnp.ones((LANE, LANE), BF16)], axis=1)
    scores = [lax.dot_general(q, k[...].astype(BF16), (((1,), (1,)), ((), ())),
                              preferred_element_type=F32) for k in k_refs]
    yield
    log_betas, stacked = [], []
    for s in scores:
        log_beta, log_1m = _log_terms(s * (1.0 / math.sqrt(HEAD_DIM)) + bias)
        log_1m = jnp.where(own, log_1m, 0.0)
        log_betas.append(log_beta)
        stacked += [log_1m[:, c * LANE:(c + 1) * LANE] for c in range(n_chunks)]
    stacked = jnp.concatenate(stacked, axis=0).astype(BF16)
    yield
    sums = jnp.dot(stacked, tri_ones, preferred_element_type=F32)
    yield
    run = run_scr[...]
    weights = []
    for r in range(len(k_refs)):
        later = [None] * n_chunks
        for c in reversed(range(n_chunks)):
            rows = sums[(r * n_chunks + c) * N_HEADS:(r * n_chunks + c + 1) * N_HEADS]
            later[c] = rows[:, :LANE] + run
            run = run + rows[:, LANE:]
        w = jnp.exp(log_betas[r] + jnp.concatenate(later, axis=1))
        weights.append(jnp.where(own, w, 0.0).astype(BF16))
    run_scr[...] = run
    yield
    acc = acc_scr[...]
    for w, v in zip(weights, v_refs):
        acc = acc + jnp.dot(w, v[...].astype(BF16), preferred_element_type=F32)
    acc_scr[...] = acc


def _interleave(generators, in_flight):
    waiting, active = list(generators), []
    while waiting or active:
        active = [g for g in active if next(g, "done") != "done"]
        if waiting and len(active) < in_flight:
            g = waiting.pop(0)
            next(g)
            active.append(g)


def _attn_kernel(pt_ref, bias_ref, q_ref, k_ref, v_ref, qs_ref, bias_col_ref, ck_ref, cv_ref,
                 o_ref, os_ref, k_buf, v_buf, sem, run_scr, acc_scr,
                 *, blk, groups, steps_per_seq, first_page, in_flight):
    pages_per_step = k_buf.shape[1]
    n_pages = pages_per_step * steps_per_seq
    group = pl.program_id(2)
    step = (pl.program_id(0) * pl.num_programs(1) + pl.program_id(1)) * groups + group
    n_steps = pl.num_programs(0) * pl.num_programs(1) * groups
    within = lax.rem(step, steps_per_seq)
    slot = lax.rem(step, 2)
    n_blocks = q_ref.shape[0] // blk

    def page_copies(of_step, to_slot):
        seq_id = lax.div(of_step, steps_per_seq)
        part = lax.rem(of_step, steps_per_seq)
        copies = []
        for r in range(pages_per_step):
            page = first_page + pt_ref[seq_id, n_pages - 1 - (part * pages_per_step + r)]
            copies.append((pltpu.make_async_copy(ck_ref.at[page], k_buf.at[to_slot, r], sem.at[to_slot, 0]),
                           pltpu.make_async_copy(cv_ref.at[page], v_buf.at[to_slot, r], sem.at[to_slot, 1])))
        return copies

    def start(copies):
        for r, (k_copy, v_copy) in enumerate(copies):
            k_copy.start(priority=int(r % 4 == 0))
            v_copy.start(priority=1)

    @pl.when(step == 0)
    def _():
        start(page_copies(step, slot))

    @pl.when(step + 1 < n_steps)
    def _():
        start(page_copies(step + 1, 1 - slot))

    for k_copy, v_copy in page_copies(step, slot):
        k_copy.wait()
        v_copy.wait()
    k_pages = [k_buf.at[slot, r] for r in range(pages_per_step)]
    v_pages = [v_buf.at[slot, r] for r in range(pages_per_step)]

    @pl.when(within == 0)
    def _():
        run_scr[...] = jnp.zeros_like(run_scr)
        acc_scr[...] = jnp.zeros_like(acc_scr)

    bias = bias_ref[pl.program_id(1)]
    tri = _strict_lower(blk)
    row = lax.broadcasted_iota(jnp.int32, (blk, blk), 0)
    col = lax.broadcasted_iota(jnp.int32, (blk, blk), 1)
    causal = col < row

    def body(p):
        mine = [a for a in reversed(range(n_blocks)) if a % (2 * groups) in (p, 2 * groups - 1 - p)]
        _interleave(
            [_sample_pages(qs_ref[0], bias_col_ref[...], k_pages, v_pages, run_scr, acc_scr)]
            + [_attn_rows(q_ref, k_ref, v_ref, o_ref, bias, tri, causal, a, blk) for a in mine],
            in_flight)

    for p in range(groups):
        pl.when(group == p)(functools.partial(body, p))

    @pl.when(within == steps_per_seq - 1)
    def _():
        os_ref[0] = acc_scr[...].astype(BF16)


def _attn(proj, sb_bias, batch, seq, q_s, cache_k, cache_v, layer, page_table, blk, groups, in_flight):
    db = q_s.shape[0]
    depth, n_pool, page, heads, d = cache_k.shape
    n_pages = page_table.shape[1]
    steps = batch * heads * groups
    assert (seq // blk) % (2 * groups) == 0 and (db * n_pages) % steps == 0
    pages_per_step = db * n_pages // steps
    assert n_pages % pages_per_step == 0
    steps_per_seq = n_pages // pages_per_step
    ck = cache_k.reshape(depth * n_pool, page * heads, d)
    cv = cache_v.reshape(depth * n_pool, page * heads, d)
    q_col, k_col, v_col = (2 * A_WIDTH) // LANE, (2 * A_WIDTH + SB_WIDTH) // LANE, (2 * A_WIDTH + 2 * SB_WIDTH) // LANE

    def head_spec(col0):
        return pl.BlockSpec((seq, HEAD_DIM), lambda b, h, p, pt, bias: (b, col0 + h))

    def seq_spec():
        return pl.BlockSpec((1, heads, d), lambda b, h, p, pt, bias:
                            (lax.div((b * heads + h) * groups + p, steps_per_seq), 0, 0))

    page_buf = pltpu.VMEM((2, pages_per_step, page * heads, d), F32)
    grid_spec = pltpu.PrefetchScalarGridSpec(
        num_scalar_prefetch=2,
        grid=(batch, heads, groups),
        in_specs=[head_spec(q_col), head_spec(k_col), head_spec(v_col), seq_spec(),
                  pl.BlockSpec((heads, 1), lambda b, h, p, pt, bias: (0, 0)),
                  pl.BlockSpec(memory_space=pl.ANY), pl.BlockSpec(memory_space=pl.ANY)],
        out_specs=[head_spec(0), seq_spec()],
        scratch_shapes=[page_buf, page_buf, pltpu.SemaphoreType.DMA((2, 2)),
                        pltpu.VMEM((heads, LANE), F32), pltpu.VMEM((heads, d), F32)],
    )
    return pl.pallas_call(
        functools.partial(_attn_kernel, blk=blk, groups=groups, steps_per_seq=steps_per_seq,
                          first_page=layer * n_pool, in_flight=in_flight),
        grid_spec=grid_spec,
        out_shape=[jax.ShapeDtypeStruct((batch * seq, SB_WIDTH), BF16),
                   jax.ShapeDtypeStruct((db, heads, d), BF16)],
        compiler_params=_params(("arbitrary", "arbitrary", "arbitrary")),
    )(page_table, sb_bias, proj, proj, proj, q_s, sb_bias.reshape(heads, 1), ck, cv)


def _mix_kernel(u_ref, va_ref, b_ref, ga0_ref, ga1_ref, gb0_ref, gb1_ref, x_ref, gt_ref, gpost_ref,
                ws_ref, bs_ref, wa_ref, wb_ref, wo_ref, *rest, chunked):
    n_side = len(rest) // 2
    o_ref = rest[n_side]
    for src, dst in zip(rest[:n_side], rest[n_side + 1:]):
        dst[...] = src[...].astype(BF16)
    bm = u_ref.shape[0]
    u = u_ref[...].astype(F32)
    if chunked:
        row = lax.broadcasted_iota(jnp.int32, (CHUNK, CHUNK), 0)
        col = lax.broadcasted_iota(jnp.int32, (CHUNK, CHUNK), 1)
        keep = col <= row
        cols = []
        for g in range(A_GROUPS):
            w = jnp.where(keep, ws_ref[g], 0.0).astype(BF16)
            b = bs_ref[g]
            rows = []
            for c in range(bm // CHUNK):
                v = va_ref[c * CHUNK:(c + 1) * CHUNK, g * CHUNK:(g + 1) * CHUNK]
                rows.append(jnp.dot(w, v, preferred_element_type=F32) + b)
            cols.append(jnp.concatenate(rows, axis=0))
        s = jnp.concatenate(cols, axis=1)
    else:
        s = va_ref[...].astype(F32) * ws_ref[...] + bs_ref[...]
    a = (u * s).astype(BF16)
    p = jnp.dot(a, wa_ref[...], preferred_element_type=F32)
    q = jnp.dot(b_ref[...], wb_ref[...], preferred_element_type=F32)
    half = p.shape[1] // 2
    merged = jnp.concatenate([
        ga0_ref[...].astype(F32) * p[:, :half] + gb0_ref[...].astype(F32) * q[:, :half],
        ga1_ref[...].astype(F32) * p[:, half:] + gb1_ref[...].astype(F32) * q[:, half:],
    ], axis=1).astype(BF16)
    mix = jnp.dot(merged, wo_ref[...], preferred_element_type=F32)
    o_ref[...] = x_ref[...] + gt_ref[0] * _rms(mix, gpost_ref[...])


def _mix(proj, b_out, x, gt, g_post, ws, bs, wa, wb, wo, bm, chunked, round_on_the_side=()):
    m, d = x.shape
    steps = m // bm
    tiles_per_group = steps // gt.shape[0]
    r = gt.shape[1]

    def col_spec(c):
        return pl.BlockSpec((bm, COL_TILE), lambda i: (i, c))

    def whole(a):
        return pl.BlockSpec(a.shape, lambda i: (0,) * a.ndim)

    slabs = [_slab_spec(a, steps, lambda i: i) for a in round_on_the_side]
    out = pl.pallas_call(
        functools.partial(_mix_kernel, chunked=chunked),
        grid=(steps,),
        in_specs=[
            col_spec(0), col_spec(1),
            pl.BlockSpec((bm, SB_WIDTH), lambda i: (i, 0)),
            col_spec(5), col_spec(6), col_spec(7), col_spec(8),
            pl.BlockSpec((bm, d), lambda i: (i, 0)),
            pl.BlockSpec((1, r, d), lambda i: (i // tiles_per_group, 0, 0)),
            pl.BlockSpec((1, d), lambda i: (0, 0)),
            whole(ws), whole(bs), whole(wa), whole(wb), whole(wo),
        ] + slabs,
        out_specs=[pl.BlockSpec((bm, d), lambda i: (i, 0))] + slabs,
        out_shape=[jax.ShapeDtypeStruct((m, d), F32)]
        + [jax.ShapeDtypeStruct(a.shape, BF16) for a in round_on_the_side],
        compiler_params=_params(("arbitrary",)),
    )(proj, proj, b_out, proj, proj, proj, proj, x, gt, g_post.reshape(1, d), ws, bs, wa, wb, wo,
      *round_on_the_side)
    return out[0], out[1:]


def _ffn_kernel(x_ref, sc_ref, sh_ref, gt_ref, gpre_ref, gpost_ref, wg_ref, wu_ref, wo_ref, o_ref,
                h_scr, acc_scr):
    j = pl.program_id(1)
    last = pl.num_programs(1) - 1
    bm = x_ref.shape[0]
    row_blocks = [slice(r, r + min(bm, ROW_SPLIT)) for r in range(0, bm, min(bm, ROW_SPLIT))]

    def rows_of(ref, rs):
        return ref[0] if ref.shape[1] == 1 else ref[0, rs]

    def weights():
        return wg_ref[...], wu_ref[...], wo_ref[...]

    def partial_out(h, w):
        gate = jnp.dot(h, w[0], preferred_element_type=F32)
        up = jnp.dot(h, w[1], preferred_element_type=F32)
        act = (gate * _sigmoid(gate) * up).astype(BF16)
        return jnp.dot(act, w[2], preferred_element_type=F32)

    @pl.when(j == 0)
    def _():
        w = weights()
        for rs in row_blocks:
            h = (_rms(x_ref[rs, :], gpre_ref[...]) * (1.0 + rows_of(sc_ref, rs))
                 + rows_of(sh_ref, rs)).astype(BF16)
            h_scr[rs, :] = h
            acc_scr[rs, :] = partial_out(h, w)

    @pl.when(jnp.logical_and(j > 0, j < last))
    def _():
        acc_scr[...] += partial_out(h_scr[...], weights())

    @pl.when(j == last)
    def _():
        w = weights()
        for rs in row_blocks:
            f = acc_scr[rs, :] + partial_out(h_scr[rs, :], w)
            o_ref[rs, :] = x_ref[rs, :] + rows_of(gt_ref, rs) * _rms(f, gpost_ref[...])


def _ffn(x, sc, sh, gt, g_pre, g_post, w_ffn_in, w_ffn_out, bm, tf):
    m, d = x.shape
    d_ff = w_ffn_out.shape[0]
    nf = d_ff // tf
    assert nf >= 2, "the kernel treats the first and last d_ff steps separately"
    tiles_per_group = m // bm // sc.shape[0]
    r = sc.shape[1]
    mod_spec = pl.BlockSpec((1, r, d), lambda i, j: (i // tiles_per_group, 0, 0))
    vec_spec = pl.BlockSpec((1, d), lambda i, j: (0, 0))
    return pl.pallas_call(
        _ffn_kernel,
        grid=(m // bm, nf),
        in_specs=[
            pl.BlockSpec((bm, d), lambda i, j: (i, 0)),
            mod_spec, mod_spec, mod_spec, vec_spec, vec_spec,
            pl.BlockSpec((d, tf), lambda i, j: (0, j)),
            pl.BlockSpec((d, tf), lambda i, j: (0, nf + j)),
            pl.BlockSpec((tf, d), lambda i, j: (j, 0)),
        ],
        out_specs=pl.BlockSpec((bm, d), lambda i, j: (i, 0)),
        out_shape=jax.ShapeDtypeStruct((m, d), F32),
        scratch_shapes=[pltpu.VMEM((bm, d), BF16), pltpu.VMEM((bm, d), F32)],
        compiler_params=_params(("arbitrary", "arbitrary")),
    )(x, sc, sh, gt, g_pre.reshape(1, d), g_post.reshape(1, d), w_ffn_in, w_ffn_in, w_ffn_out)


def kernel(x_prompt, x_sample, c_prompt, c_sample, cache_k, cache_v, page_table, w_ada, b_ada, g_pre_mix, g_post_mix, w_in, ln_v_g, ln_v_b, w_s, b_s, sb_bias, w_branch_a, w_branch_b, w_out, g_pre_ffn, g_post_ffn, w_ffn_in, w_ffn_out):
    batch, seq, d = x_prompt.shape
    db, dseq, _ = x_sample.shape
    depth = w_ada.shape[0]
    assert dseq == 1, "the sample group decodes one token per sequence"
    mp, ms = batch * seq, db * dseq
    yp = x_prompt.reshape(mp, d)
    ys = x_sample.reshape(ms, d)
    c_all = jnp.concatenate([c_prompt, c_sample], axis=0)
    outs = {name: [] for name in ("kp", "vp", "ks", "vs", "cv")}

    for l in range(depth):
        mods = _ada(c_all, w_ada[l], b_ada[l])
        mods_p = [m.reshape(batch, 1, d) for m in jnp.split(mods[:batch], 6, axis=-1)]
        mods_s = [m.reshape(1, db, d) for m in jnp.split(mods[batch:], 6, axis=-1)]

        sh1, sc1, gt1, sh2, sc2, gt2 = mods_p
        sh1_s, sc1_s, gt1_s, sh2_s, sc2_s, gt2_s = mods_s
        proj_s, k_s, v_s, cv_s, w_in_l, _ = _inproj(
            ys, sc1_s, sh1_s, g_pre_mix[l], w_in[l], ln_v_g[l], ln_v_b[l], bm=ms, want_va=True)
        proj, kp, vp, (wa_l, wb_l, wo_l) = _inproj(
            yp, sc1, sh1, g_pre_mix[l], w_in_l, ln_v_g[l], ln_v_b[l], bm=INPROJ_ROWS, want_va=False,
            round_on_the_side=(w_branch_a[l], w_branch_b[l], w_out[l]))
        q_s = proj_s[:, 2 * A_WIDTH:2 * A_WIDTH + SB_WIDTH].reshape(ms, N_HEADS, HEAD_DIM)
        b_out, b_out_s = _attn(proj, sb_bias[l], batch, seq, q_s, cache_k, cache_v, l, page_table,
                               blk=ATTN_BLOCK, groups=ATTN_GROUPS, in_flight=ATTN_IN_FLIGHT)

        x1, (wfi_l, wfo_l) = _mix(proj, b_out, yp, gt1, g_post_mix[l], w_s[l],
                                  b_s[l].reshape(A_GROUPS, CHUNK, 1), wa_l, wb_l, wo_l, bm=MIX_ROWS,
                                  chunked=True, round_on_the_side=(w_ffn_in[l], w_ffn_out[l]))
        yp = _ffn(x1, sc2, sh2, gt2, g_pre_ffn[l], g_post_ffn[l], wfi_l, wfo_l,
                  bm=FFN_ROWS, tf=FFN_COLS)
        outs["kp"].append(kp.reshape(batch, seq, N_HEADS, HEAD_DIM))
        outs["vp"].append(vp.reshape(batch, seq, N_HEADS, HEAD_DIM))

        ws_row = jnp.repeat(w_s[l][:, 0, 0], CHUNK).reshape(1, A_WIDTH)
        bs_row = jnp.repeat(b_s[l][:, 0], CHUNK).reshape(1, A_WIDTH)
        x1, _ = _mix(proj_s, b_out_s.reshape(ms, SB_WIDTH), ys, gt1_s, g_post_mix[l], ws_row, bs_row,
                     wa_l, wb_l, wo_l, bm=ms, chunked=False)
        ys = _ffn(x1, sc2_s, sh2_s, gt2_s, g_pre_ffn[l], g_post_ffn[l], wfi_l, wfo_l,
                  bm=ms, tf=FFN_COLS_SAMPLE)
        outs["ks"].append(k_s.reshape(db, dseq, N_HEADS, HEAD_DIM))
        outs["vs"].append(v_s.reshape(db, dseq, N_HEADS, HEAD_DIM))
        outs["cv"].append(cv_s.reshape(db, dseq, A_WIDTH))

    return (yp.reshape(batch, seq, d), ys.reshape(db, dseq, d),
            jnp.stack(outs["kp"]), jnp.stack(outs["vp"]),
            jnp.stack(outs["ks"]), jnp.stack(outs["vs"]), jnp.stack(outs["cv"]))
```

```python
import functools
import math

import jax
import jax.numpy as jnp
from jax import lax
from jax.experimental import pallas as pl
from jax.experimental.pallas import tpu as pltpu

F32 = jnp.float32
BF16 = jnp.bfloat16

EPS = 1e-6
LANE = 128
HEAD_DIM = 128
N_HEADS = 8
A_GROUPS = 8
CHUNK = 128
A_WIDTH = A_GROUPS * CHUNK
SB_WIDTH = N_HEADS * HEAD_DIM
COL_TILE = 1024
MXU_COLS = 256
ROW_SPLIT = 256
BF16_ROWS = 16
VMEM_LIMIT = 60 * 1024 * 1024

INPROJ_ROWS = 1024
MIX_ROWS = 256
FFN_ROWS = 1024
FFN_COLS = 256
FFN_COLS_SAMPLE = 512
ATTN_BLOCK = 256
ATTN_GROUPS = 2
ATTN_IN_FLIGHT = 4


def _params(semantics):
    return pltpu.CompilerParams(dimension_semantics=semantics, vmem_limit_bytes=VMEM_LIMIT)


def _rms(x, g):
    return x * lax.rsqrt(jnp.mean(x * x, axis=-1, keepdims=True) + EPS) * g


def _sigmoid(x):
    return 1.0 / (1.0 + jnp.exp(-x))


def _ada_kernel(c_ref, w_ref, b_ref, o_ref):
    c = c_ref[...]
    a = (c * _sigmoid(c)).astype(BF16)
    o_ref[...] = jnp.dot(a, w_ref[...].astype(BF16), preferred_element_type=F32) + b_ref[...]


def _ada(c_all, w_ada, b_ada):
    rows, d = c_all.shape
    n = w_ada.shape[1]
    tn = 1024
    return pl.pallas_call(
        _ada_kernel,
        grid=(n // tn,),
        in_specs=[
            pl.BlockSpec((rows, d), lambda j: (0, 0)),
            pl.BlockSpec((d, tn), lambda j: (0, j)),
            pl.BlockSpec((1, tn), lambda j: (0, j)),
        ],
        out_specs=pl.BlockSpec((rows, tn), lambda j: (0, j)),
        out_shape=jax.ShapeDtypeStruct((rows, n), F32),
        compiler_params=_params(("arbitrary",)),
    )(c_all, w_ada, b_ada.reshape(1, n))


def _inproj_kernel(x_ref, sc_ref, sh_ref, g_ref, w_ref, lng_ref, lnb_ref, *rest, want_va, n_side):
    side_in, outs, h_scr = rest[:n_side], rest[n_side:-1], rest[-1]
    proj_ref, k_ref, v_ref = outs[:3]
    va_ref = outs[3] if want_va else None
    round_w = w_ref.dtype != BF16
    w16_ref = outs[3 + want_va] if round_w else None
    side_out = outs[3 + want_va + round_w:]
    j = pl.program_id(1)
    bm = x_ref.shape[0]
    col_chunks = [slice(c, c + MXU_COLS) for c in range(0, COL_TILE, MXU_COLS)]

    def weight_chunks():
        if not round_w:
            return [w_ref[:, cs] for cs in col_chunks]
        ws = [w_ref[:, cs].astype(BF16) for cs in col_chunks]
        for cs, w in zip(col_chunks, ws):
            w16_ref[:, cs] = w
        return ws

    def chunk(h, w):
        return jnp.dot(h, w, preferred_element_type=F32)

    @pl.when(j == 0)
    def _():
        for src, dst in zip(side_in, side_out):
            dst[...] = src[...].astype(BF16)
        ws = weight_chunks()
        for r in range(0, bm, min(bm, ROW_SPLIT)):
            rs = slice(r, r + min(bm, ROW_SPLIT))
            sc = sc_ref[0] if sc_ref.shape[1] == 1 else sc_ref[0, rs]
            sh = sh_ref[0] if sh_ref.shape[1] == 1 else sh_ref[0, rs]
            h = (_rms(x_ref[rs, :], g_ref[...]) * (1.0 + sc) + sh).astype(BF16)
            h_scr[rs, :] = h
            for cs, w in zip(col_chunks, ws):
                proj_ref[rs, cs] = jax.nn.gelu(chunk(h, w)).astype(BF16)

    @pl.when(j == 1)
    def _():
        ws = weight_chunks()
        for r in range(0, bm, min(bm, ROW_SPLIT)):
            rs = slice(r, r + min(bm, ROW_SPLIT))
            h = h_scr[rs, :]
            gs = [jax.nn.gelu(chunk(h, w)) for w in ws]
            mean = sum(jnp.sum(g, axis=-1, keepdims=True) for g in gs) * (1.0 / COL_TILE)
            xcs = [g - mean for g in gs]
            var = sum(jnp.sum(xc * xc, axis=-1, keepdims=True) for xc in xcs) * (1.0 / COL_TILE)
            inv = lax.rsqrt(var + EPS)
            for cs, xc in zip(col_chunks, xcs):
                y = xc * inv * lng_ref[:, cs] + lnb_ref[:, cs]
                if va_ref is not None:
                    va_ref[rs, cs] = y
                proj_ref[rs, cs] = y.astype(BF16)

    def raw(f32_ref):
        h = h_scr[...]
        for cs, w in zip(col_chunks, weight_chunks()):
            acc = chunk(h, w)
            if f32_ref is not None:
                f32_ref[:, cs] = acc
            proj_ref[:, cs] = acc.astype(BF16)

    pl.when(j == 2)(lambda: raw(None))
    pl.when(j == 3)(lambda: raw(k_ref))
    pl.when(j == 4)(lambda: raw(v_ref))

    @pl.when(j >= 5)
    def _():
        h = h_scr[...]
        for cs, w in zip(col_chunks, weight_chunks()):
            proj_ref[:, cs] = _sigmoid(chunk(h, w)).astype(BF16)


def _slab_spec(a, steps, index_of):
    share = 1
    while (a.shape[0] * share) % (steps * BF16_ROWS):
        share *= 2
    assert steps % share == 0
    return pl.BlockSpec((a.shape[0] * share // steps, a.shape[1]),
                        lambda *grid: (index_of(*grid) // share, 0))


def _inproj(x, sc, sh, g_pre, w_in, ln_g, ln_b, bm, want_va, round_on_the_side=()):
    m, d = x.shape
    n_in = w_in.shape[1]
    tiles = m // bm
    tiles_per_group = tiles // sc.shape[0]
    r = sc.shape[1]
    mod_spec = pl.BlockSpec((1, r, d), lambda i, j: (i // tiles_per_group, 0, 0))
    row_spec = pl.BlockSpec((bm, COL_TILE), lambda i, j: (i, 0))
    w_spec = pl.BlockSpec((d, COL_TILE), lambda i, j: (0, j))
    n_f32 = 3 if want_va else 2
    round_w = w_in.dtype != BF16
    assert not round_w or tiles == 1, "the weight copy is written once per column tile"
    slabs = [_slab_spec(a, tiles, lambda i, j: i) for a in round_on_the_side]
    out = pl.pallas_call(
        functools.partial(_inproj_kernel, want_va=want_va, n_side=len(slabs)),
        grid=(tiles, n_in // COL_TILE),
        in_specs=[
            pl.BlockSpec((bm, d), lambda i, j: (i, 0)),
            mod_spec, mod_spec,
            pl.BlockSpec((1, d), lambda i, j: (0, 0)),
            w_spec,
            pl.BlockSpec((1, A_WIDTH), lambda i, j: (0, 0)),
            pl.BlockSpec((1, A_WIDTH), lambda i, j: (0, 0)),
        ] + slabs,
        out_specs=[pl.BlockSpec((bm, COL_TILE), lambda i, j: (i, j))] + [row_spec] * n_f32
        + [w_spec] * round_w + slabs,
        out_shape=[jax.ShapeDtypeStruct((m, n_in), BF16)]
        + [jax.ShapeDtypeStruct((m, COL_TILE), F32)] * n_f32
        + [jax.ShapeDtypeStruct(w_in.shape, BF16)] * round_w
        + [jax.ShapeDtypeStruct(a.shape, BF16) for a in round_on_the_side],
        scratch_shapes=[pltpu.VMEM((bm, d), BF16)],
        compiler_params=_params(("arbitrary", "arbitrary")),
    )(x, sc, sh, g_pre.reshape(1, d), w_in, ln_g.reshape(1, A_WIDTH), ln_b.reshape(1, A_WIDTH),
      *round_on_the_side)
    n_main = 1 + n_f32 + round_w
    return (*out[:n_main], tuple(out[n_main:]))


def _log_terms(z):
    soft = jnp.log(1.0 + jnp.exp(-jnp.abs(z)))
    log_beta = jnp.minimum(z, 0.0) - soft
    return log_beta, log_beta - z


def _strict_lower(n):
    row = lax.broadcasted_iota(jnp.int32, (n, n), 0)
    col = lax.broadcasted_iota(jnp.int32, (n, n), 1)
    return (row > col).astype(BF16)


MASKED_LOG = -1e30


def _attn_rows(q_ref, k_ref, v_ref, o_ref, bias, tri, causal, a, blk):
    n_blocks = a + 1
    n = n_blocks * blk
    q = q_ref[a * blk:(a + 1) * blk, :]
    s = lax.dot_general(q, k_ref[0:n, :], (((1,), (1,)), ((), ())), preferred_element_type=F32)
    yield
    z = s * (1.0 / math.sqrt(HEAD_DIM)) + bias
    log_beta, log_1m = _log_terms(z)
    lbs = [log_beta[:, j * blk:(j + 1) * blk] for j in range(n_blocks)]
    l1s = [log_1m[:, j * blk:(j + 1) * blk] for j in range(n_blocks)]
    lbs[-1] = jnp.where(causal, lbs[-1], MASKED_LOG)
    l1s[-1] = jnp.where(causal, l1s[-1], 0.0)
    stacked = jnp.concatenate(l1s, axis=0).astype(BF16)
    yield
    excl = jnp.dot(stacked, tri, preferred_element_type=F32)
    yield
    excl = [excl[j * blk:(j + 1) * blk] for j in range(n_blocks)]
    run = jnp.zeros((blk, 1), F32)
    ws = [None] * n_blocks
    for j in reversed(range(n_blocks)):
        ws[j] = jnp.exp(lbs[j] + excl[j] + run).astype(BF16)
        run = run + excl[j][:, :1] + l1s[j][:, :1]
    w = jnp.concatenate(ws, axis=1)
    yield
    out = jnp.dot(w, v_ref[0:n, :], preferred_element_type=F32)
    o_ref[a * blk:(a + 1) * blk, :] = out.astype(BF16)


def _sample_pages(q, bias, k_refs, v_refs, run_scr, acc_scr, wait_keys, wait_values):
    wait_keys()
    page_cols = k_refs[0].shape[0]
    n_chunks = page_cols // LANE
    head = lax.broadcasted_iota(jnp.int32, (N_HEADS, page_cols), 0)
    col = lax.broadcasted_iota(jnp.int32, (N_HEADS, page_cols), 1)
    own = (col % N_HEADS) == head
    tri_ones = jnp.concatenate([_strict_lower(LANE), jnp.ones((LANE, LANE), BF16)], axis=1)
    scores = [lax.dot_general(q, k[...].astype(BF16), (((1,), (1,)), ((), ())),
                              preferred_element_type=F32) for k in k_refs]
    yield
    log_betas, stacked = [], []
    for s in scores:
        log_beta, log_1m = _log_terms(s * (1.0 / math.sqrt(HEAD_DIM)) + bias)
        log_1m = jnp.where(own, log_1m, 0.0)
        log_betas.append(log_beta)
        stacked += [log_1m[:, c * LANE:(c + 1) * LANE] for c in range(n_chunks)]
    stacked = jnp.concatenate(stacked, axis=0).astype(BF16)
    yield
    sums = jnp.dot(stacked, tri_ones, preferred_element_type=F32)
    yield
    run = run_scr[...]
    weights = []
    for r in range(len(k_refs)):
        later = [None] * n_chunks
        for c in reversed(range(n_chunks)):
            rows = sums[(r * n_chunks + c) * N_HEADS:(r * n_chunks + c + 1) * N_HEADS]
            later[c] = rows[:, :LANE] + run
            run = run + rows[:, LANE:]
        w = jnp.exp(log_betas[r] + jnp.concatenate(later, axis=1))
        weights.append(jnp.where(own, w, 0.0).astype(BF16))
    run_scr[...] = run
    yield
    wait_values()
    acc = acc_scr[...]
    for w, v in zip(weights, v_refs):
        acc = acc + jnp.dot(w, v[...].astype(BF16), preferred_element_type=F32)
    acc_scr[...] = acc


def _interleave(generators, in_flight):
    waiting, active = list(generators), []
    while waiting or active:
        active = [g for g in active if next(g, "done") != "done"]
        if waiting and len(active) < in_flight:
            g = waiting.pop(0)
            next(g)
            active.append(g)


def _attn_kernel(pt_ref, bias_ref, q_ref, k_ref, v_ref, qs_ref, bias_col_ref, ck_ref, cv_ref,
                 o_ref, os_ref, k_buf, v_buf, sem, run_scr, acc_scr,
                 *, blk, groups, steps_per_seq, first_page, in_flight):
    pages_per_step = k_buf.shape[1]
    n_pages = pages_per_step * steps_per_seq
    group = pl.program_id(2)
    step = (pl.program_id(0) * pl.num_programs(1) + pl.program_id(1)) * groups + group
    n_steps = pl.num_programs(0) * pl.num_programs(1) * groups
    within = lax.rem(step, steps_per_seq)
    slot = lax.rem(step, 2)
    n_blocks = q_ref.shape[0] // blk

    def page_copies(of_step, to_slot):
        seq_id = lax.div(of_step, steps_per_seq)
        part = lax.rem(of_step, steps_per_seq)
        copies = []
        for r in range(pages_per_step):
            page = first_page + pt_ref[seq_id, n_pages - 1 - (part * pages_per_step + r)]
            copies.append((pltpu.make_async_copy(ck_ref.at[page], k_buf.at[to_slot, r], sem.at[to_slot, 0]),
                           pltpu.make_async_copy(cv_ref.at[page], v_buf.at[to_slot, r], sem.at[to_slot, 1])))
        return copies

    def start(copies):
        for k_copy, v_copy in copies:
            k_copy.start(priority=0)
            v_copy.start(priority=1)

    @pl.when(step == 0)
    def _():
        start(page_copies(step, slot))

    @pl.when(step + 1 < n_steps)
    def _():
        start(page_copies(step + 1, 1 - slot))

    arriving = page_copies(step, slot)

    def wait_keys():
        for k_copy, _ in arriving:
            k_copy.wait()

    def wait_values():
        for _, v_copy in arriving:
            v_copy.wait()

    k_pages = [k_buf.at[slot, r] for r in range(pages_per_step)]
    v_pages = [v_buf.at[slot, r] for r in range(pages_per_step)]

    @pl.when(within == 0)
    def _():
        run_scr[...] = jnp.zeros_like(run_scr)
        acc_scr[...] = jnp.zeros_like(acc_scr)

    bias = bias_ref[pl.program_id(1)]
    tri = _strict_lower(blk)
    row = lax.broadcasted_iota(jnp.int32, (blk, blk), 0)
    col = lax.broadcasted_iota(jnp.int32, (blk, blk), 1)
    causal = col < row

    def body(p):
        mine = [a for a in reversed(range(n_blocks)) if a % (2 * groups) in (p, 2 * groups - 1 - p)]
        _interleave(
            [_attn_rows(q_ref, k_ref, v_ref, o_ref, bias, tri, causal, mine[0], blk),
             _sample_pages(qs_ref[0], bias_col_ref[...], k_pages, v_pages, run_scr, acc_scr,
                           wait_keys, wait_values)]
            + [_attn_rows(q_ref, k_ref, v_ref, o_ref, bias, tri, causal, a, blk) for a in mine[1:]],
            in_flight)

    for p in range(groups):
        pl.when(group == p)(functools.partial(body, p))

    @pl.when(within == steps_per_seq - 1)
    def _():
        os_ref[0] = acc_scr[...].astype(BF16)


def _attn(proj, sb_bias, batch, seq, q_s, cache_k, cache_v, layer, page_table, blk, groups, in_flight):
    db = q_s.shape[0]
    depth, n_pool, page, heads, d = cache_k.shape
    n_pages = page_table.shape[1]
    steps = batch * heads * groups
    assert (seq // blk) % (2 * groups) == 0 and (db * n_pages) % steps == 0
    pages_per_step = db * n_pages // steps
    assert n_pages % pages_per_step == 0
    steps_per_seq = n_pages // pages_per_step
    ck = cache_k.reshape(depth * n_pool, page * heads, d)
    cv = cache_v.reshape(depth * n_pool, page * heads, d)
    q_col, k_col, v_col = (2 * A_WIDTH) // LANE, (2 * A_WIDTH + SB_WIDTH) // LANE, (2 * A_WIDTH + 2 * SB_WIDTH) // LANE

    def head_spec(col0):
        return pl.BlockSpec((seq, HEAD_DIM), lambda b, h, p, pt, bias: (b, col0 + h))

    def seq_spec():
        return pl.BlockSpec((1, heads, d), lambda b, h, p, pt, bias:
                            (lax.div((b * heads + h) * groups + p, steps_per_seq), 0, 0))

    page_buf = pltpu.VMEM((2, pages_per_step, page * heads, d), F32)
    grid_spec = pltpu.PrefetchScalarGridSpec(
        num_scalar_prefetch=2,
        grid=(batch, heads, groups),
        in_specs=[head_spec(q_col), head_spec(k_col), head_spec(v_col), seq_spec(),
                  pl.BlockSpec((heads, 1), lambda b, h, p, pt, bias: (0, 0)),
                  pl.BlockSpec(memory_space=pl.ANY), pl.BlockSpec(memory_space=pl.ANY)],
        out_specs=[head_spec(0), seq_spec()],
        scratch_shapes=[page_buf, page_buf, pltpu.SemaphoreType.DMA((2, 2)),
                        pltpu.VMEM((heads, LANE), F32), pltpu.VMEM((heads, d), F32)],
    )
    return pl.pallas_call(
        functools.partial(_attn_kernel, blk=blk, groups=groups, steps_per_seq=steps_per_seq,
                          first_page=layer * n_pool, in_flight=in_flight),
        grid_spec=grid_spec,
        out_shape=[jax.ShapeDtypeStruct((batch * seq, SB_WIDTH), BF16),
                   jax.ShapeDtypeStruct((db, heads, d), BF16)],
        compiler_params=_params(("arbitrary", "arbitrary", "arbitrary")),
    )(page_table, sb_bias, proj, proj, proj, q_s, sb_bias.reshape(heads, 1), ck, cv)


def _mix_kernel(u_ref, va_ref, b_ref, ga0_ref, ga1_ref, gb0_ref, gb1_ref, x_ref, gt_ref, gpost_ref,
                ws_ref, bs_ref, wa_ref, wb_ref, wo_ref, *rest, chunked):
    n_side = len(rest) // 2
    o_ref = rest[n_side]
    for src, dst in zip(rest[:n_side], rest[n_side + 1:]):
        dst[...] = src[...].astype(BF16)
    bm = u_ref.shape[0]
    u = u_ref[...].astype(F32)
    if chunked:
        row = lax.broadcasted_iota(jnp.int32, (CHUNK, CHUNK), 0)
        col = lax.broadcasted_iota(jnp.int32, (CHUNK, CHUNK), 1)
        keep = col <= row
        cols = []
        for g in range(A_GROUPS):
            w = jnp.where(keep, ws_ref[g], 0.0).astype(BF16)
            b = bs_ref[g]
            rows = []
            for c in range(bm // CHUNK):
                v = va_ref[c * CHUNK:(c + 1) * CHUNK, g * CHUNK:(g + 1) * CHUNK]
                rows.append(jnp.dot(w, v, preferred_element_type=F32) + b)
            cols.append(jnp.concatenate(rows, axis=0))
        s = jnp.concatenate(cols, axis=1)
    else:
        s = va_ref[...].astype(F32) * ws_ref[...] + bs_ref[...]
    a = (u * s).astype(BF16)
    p = jnp.dot(a, wa_ref[...], preferred_element_type=F32)
    q = jnp.dot(b_ref[...], wb_ref[...], preferred_element_type=F32)
    half = p.shape[1] // 2
    merged = jnp.concatenate([
        ga0_ref[...].astype(F32) * p[:, :half] + gb0_ref[...].astype(F32) * q[:, :half],
        ga1_ref[...].astype(F32) * p[:, half:] + gb1_ref[...].astype(F32) * q[:, half:],
    ], axis=1).astype(BF16)
    mix = jnp.dot(merged, wo_ref[...], preferred_element_type=F32)
    o_ref[...] = x_ref[...] + gt_ref[0] * _rms(mix, gpost_ref[...])


def _mix(proj, b_out, x, gt, g_post, ws, bs, wa, wb, wo, bm, chunked, round_on_the_side=()):
    m, d = x.shape
    steps = m // bm
    tiles_per_group = steps // gt.shape[0]
    r = gt.shape[1]

    def col_spec(c):
        return pl.BlockSpec((bm, COL_TILE), lambda i: (i, c))

    def whole(a):
        return pl.BlockSpec(a.shape, lambda i: (0,) * a.ndim)

    slabs = [_slab_spec(a, steps, lambda i: i) for a in round_on_the_side]
    out = pl.pallas_call(
        functools.partial(_mix_kernel, chunked=chunked),
        grid=(steps,),
        in_specs=[
            col_spec(0), col_spec(1),
            pl.BlockSpec((bm, SB_WIDTH), lambda i: (i, 0)),
            col_spec(5), col_spec(6), col_spec(7), col_spec(8),
            pl.BlockSpec((bm, d), lambda i: (i, 0)),
            pl.BlockSpec((1, r, d), lambda i: (i // tiles_per_group, 0, 0)),
            pl.BlockSpec((1, d), lambda i: (0, 0)),
            whole(ws), whole(bs), whole(wa), whole(wb), whole(wo),
        ] + slabs,
        out_specs=[pl.BlockSpec((bm, d), lambda i: (i, 0))] + slabs,
        out_shape=[jax.ShapeDtypeStruct((m, d), F32)]
        + [jax.ShapeDtypeStruct(a.shape, BF16) for a in round_on_the_side],
        compiler_params=_params(("arbitrary",)),
    )(proj, proj, b_out, proj, proj, proj, proj, x, gt, g_post.reshape(1, d), ws, bs, wa, wb, wo,
      *round_on_the_side)
    return out[0], out[1:]


def _ffn_kernel(x_ref, sc_ref, sh_ref, gt_ref, gpre_ref, gpost_ref, wg_ref, wu_ref, wo_ref, o_ref,
                h_scr, acc_scr):
    j = pl.program_id(1)
    last = pl.num_programs(1) - 1
    bm = x_ref.shape[0]
    row_blocks = [slice(r, r + min(bm, ROW_SPLIT)) for r in range(0, bm, min(bm, ROW_SPLIT))]

    def rows_of(ref, rs):
        return ref[0] if ref.shape[1] == 1 else ref[0, rs]

    def weights():
        return wg_ref[...], wu_ref[...], wo_ref[...]

    def partial_out(h, w):
        gate = jnp.dot(h, w[0], preferred_element_type=F32)
        up = jnp.dot(h, w[1], preferred_element_type=F32)
        act = (gate * _sigmoid(gate) * up).astype(BF16)
        return jnp.dot(act, w[2], preferred_element_type=F32)

    @pl.when(j == 0)
    def _():
        w = weights()
        for rs in row_blocks:
            h = (_rms(x_ref[rs, :], gpre_ref[...]) * (1.0 + rows_of(sc_ref, rs))
                 + rows_of(sh_ref, rs)).astype(BF16)
            h_scr[rs, :] = h
            acc_scr[rs, :] = partial_out(h, w)

    @pl.when(jnp.logical_and(j > 0, j < last))
    def _():
        acc_scr[...] += partial_out(h_scr[...], weights())

    @pl.when(j == last)
    def _():
        w = weights()
        for rs in row_blocks:
            f = acc_scr[rs, :] + partial_out(h_scr[rs, :], w)
            o_ref[rs, :] = x_ref[rs, :] + rows_of(gt_ref, rs) * _rms(f, gpost_ref[...])


def _ffn(x, sc, sh, gt, g_pre, g_post, w_ffn_in, w_ffn_out, bm, tf):
    m, d = x.shape
    d_ff = w_ffn_out.shape[0]
    nf = d_ff // tf
    assert nf >= 2, "the kernel treats the first and last d_ff steps separately"
    tiles_per_group = m // bm // sc.shape[0]
    r = sc.shape[1]
    mod_spec = pl.BlockSpec((1, r, d), lambda i, j: (i // tiles_per_group, 0, 0))
    vec_spec = pl.BlockSpec((1, d), lambda i, j: (0, 0))
    return pl.pallas_call(
        _ffn_kernel,
        grid=(m // bm, nf),
        in_specs=[
            pl.BlockSpec((bm, d), lambda i, j: (i, 0)),
            mod_spec, mod_spec, mod_spec, vec_spec, vec_spec,
            pl.BlockSpec((d, tf), lambda i, j: (0, j)),
            pl.BlockSpec((d, tf), lambda i, j: (0, nf + j)),
            pl.BlockSpec((tf, d), lambda i, j: (j, 0)),
        ],
        out_specs=pl.BlockSpec((bm, d), lambda i, j: (i, 0)),
        out_shape=jax.ShapeDtypeStruct((m, d), F32),
        scratch_shapes=[pltpu.VMEM((bm, d), BF16), pltpu.VMEM((bm, d), F32)],
        compiler_params=_params(("arbitrary", "arbitrary")),
    )(x, sc, sh, gt, g_pre.reshape(1, d), g_post.reshape(1, d), w_ffn_in, w_ffn_in, w_ffn_out)


def kernel(x_prompt, x_sample, c_prompt, c_sample, cache_k, cache_v, page_table, w_ada, b_ada, g_pre_mix, g_post_mix, w_in, ln_v_g, ln_v_b, w_s, b_s, sb_bias, w_branch_a, w_branch_b, w_out, g_pre_ffn, g_post_ffn, w_ffn_in, w_ffn_out):
    batch, seq, d = x_prompt.shape
    db, dseq, _ = x_sample.shape
    depth = w_ada.shape[0]
    assert dseq == 1, "the sample group decodes one token per sequence"
    mp, ms = batch * seq, db * dseq
    yp = x_prompt.reshape(mp, d)
    ys = x_sample.reshape(ms, d)
    c_all = jnp.concatenate([c_prompt, c_sample], axis=0)
    outs = {name: [] for name in ("kp", "vp", "ks", "vs", "cv")}

    for l in range(depth):
        mods = _ada(c_all, w_ada[l], b_ada[l])
        mods_p = [m.reshape(batch, 1, d) for m in jnp.split(mods[:batch], 6, axis=-1)]
        mods_s = [m.reshape(1, db, d) for m in jnp.split(mods[batch:], 6, axis=-1)]

        sh1, sc1, gt1, sh2, sc2, gt2 = mods_p
        sh1_s, sc1_s, gt1_s, sh2_s, sc2_s, gt2_s = mods_s
        proj_s, k_s, v_s, cv_s, w_in_l, _ = _inproj(
            ys, sc1_s, sh1_s, g_pre_mix[l], w_in[l], ln_v_g[l], ln_v_b[l], bm=ms, want_va=True)
        proj, kp, vp, (wa_l, wb_l, wo_l) = _inproj(
            yp, sc1, sh1, g_pre_mix[l], w_in_l, ln_v_g[l], ln_v_b[l], bm=INPROJ_ROWS, want_va=False,
            round_on_the_side=(w_branch_a[l], w_branch_b[l], w_out[l]))
        q_s = proj_s[:, 2 * A_WIDTH:2 * A_WIDTH + SB_WIDTH].reshape(ms, N_HEADS, HEAD_DIM)
        b_out, b_out_s = _attn(proj, sb_bias[l], batch, seq, q_s, cache_k, cache_v, l, page_table,
                               blk=ATTN_BLOCK, groups=ATTN_GROUPS, in_flight=ATTN_IN_FLIGHT)

        x1, (wfi_l, wfo_l) = _mix(proj, b_out, yp, gt1, g_post_mix[l], w_s[l],
                                  b_s[l].reshape(A_GROUPS, CHUNK, 1), wa_l, wb_l, wo_l, bm=MIX_ROWS,
                                  chunked=True, round_on_the_side=(w_ffn_in[l], w_ffn_out[l]))
        yp = _ffn(x1, sc2, sh2, gt2, g_pre_ffn[l], g_post_ffn[l], wfi_l, wfo_l,
                  bm=FFN_ROWS, tf=FFN_COLS)
        outs["kp"].append(kp.reshape(batch, seq, N_HEADS, HEAD_DIM))
        outs["vp"].append(vp.reshape(batch, seq, N_HEADS, HEAD_DIM))

        ws_row = jnp.repeat(w_s[l][:, 0, 0], CHUNK).reshape(1, A_WIDTH)
        bs_row = jnp.repeat(b_s[l][:, 0], CHUNK).reshape(1, A_WIDTH)
        x1, _ = _mix(proj_s, b_out_s.reshape(ms, SB_WIDTH), ys, gt1_s, g_post_mix[l], ws_row, bs_row,
                     wa_l, wb_l, wo_l, bm=ms, chunked=False)
        ys = _ffn(x1, sc2_s, sh2_s, gt2_s, g_pre_ffn[l], g_post_ffn[l], wfi_l, wfo_l,
                  bm=ms, tf=FFN_COLS_SAMPLE)
        outs["ks"].append(k_s.reshape(db, dseq, N_HEADS, HEAD_DIM))
        outs["vs"].append(v_s.reshape(db, dseq, N_HEADS, HEAD_DIM))
        outs["cv"].append(cv_s.reshape(db, dseq, A_WIDTH))

    return (yp.reshape(batch, seq, d), ys.reshape(db, dseq, d),
            jnp.stack(outs["kp"]), jnp.stack(outs["vp"]),
            jnp.stack(outs["ks"]), jnp.stack(outs["vs"]), jnp.stack(outs["cv"]))
```

```python
import functools
import math

import jax
import jax.numpy as jnp
from jax import lax
from jax.experimental import pallas as pl
from jax.experimental.pallas import tpu as pltpu

F32 = jnp.float32
BF16 = jnp.bfloat16

EPS = 1e-6
LANE = 128
HEAD_DIM = 128
N_HEADS = 8
A_GROUPS = 8
CHUNK = 128
A_WIDTH = A_GROUPS * CHUNK
SB_WIDTH = N_HEADS * HEAD_DIM
COL_TILE = 1024
MXU_COLS = 256
ROW_SPLIT = 256
BF16_ROWS = 16
VMEM_LIMIT = 60 * 1024 * 1024

INPROJ_ROWS = 1024
MIX_ROWS = 256
FFN_ROWS = 1024
FFN_COLS = 256
FFN_COLS_SAMPLE = 512
ATTN_BLOCK = 256
ATTN_GROUPS = 2
ATTN_IN_FLIGHT = 4


def _params(semantics):
    return pltpu.CompilerParams(dimension_semantics=semantics, vmem_limit_bytes=VMEM_LIMIT)


def _rms(x, g):
    return x * lax.rsqrt(jnp.mean(x * x, axis=-1, keepdims=True) + EPS) * g


def _sigmoid(x):
    return 1.0 / (1.0 + jnp.exp(-x))


def _ada_kernel(c_ref, w_ref, b_ref, o_ref):
    c = c_ref[...]
    a = (c * _sigmoid(c)).astype(BF16)
    o_ref[...] = jnp.dot(a, w_ref[...].astype(BF16), preferred_element_type=F32) + b_ref[...]


def _ada(c_all, w_ada, b_ada):
    rows, d = c_all.shape
    n = w_ada.shape[1]
    tn = 1024
    return pl.pallas_call(
        _ada_kernel,
        grid=(n // tn,),
        in_specs=[
            pl.BlockSpec((rows, d), lambda j: (0, 0)),
            pl.BlockSpec((d, tn), lambda j: (0, j)),
            pl.BlockSpec((1, tn), lambda j: (0, j)),
        ],
        out_specs=pl.BlockSpec((rows, tn), lambda j: (0, j)),
        out_shape=jax.ShapeDtypeStruct((rows, n), F32),
        compiler_params=_params(("arbitrary",)),
    )(c_all, w_ada, b_ada.reshape(1, n))


def _inproj_kernel(x_ref, sc_ref, sh_ref, g_ref, w_ref, lng_ref, lnb_ref, *rest, want_va, n_side):
    side_in, outs, h_scr = rest[:n_side], rest[n_side:-1], rest[-1]
    proj_ref, k_ref, v_ref = outs[:3]
    va_ref = outs[3] if want_va else None
    round_w = w_ref.dtype != BF16
    w16_ref = outs[3 + want_va] if round_w else None
    side_out = outs[3 + want_va + round_w:]
    j = pl.program_id(1)
    bm = x_ref.shape[0]
    col_chunks = [slice(c, c + MXU_COLS) for c in range(0, COL_TILE, MXU_COLS)]

    def weight_chunks():
        if not round_w:
            return [w_ref[:, cs] for cs in col_chunks]
        ws = [w_ref[:, cs].astype(BF16) for cs in col_chunks]
        for cs, w in zip(col_chunks, ws):
            w16_ref[:, cs] = w
        return ws

    def chunk(h, w):
        return jnp.dot(h, w, preferred_element_type=F32)

    @pl.when(j == 0)
    def _():
        for src, dst in zip(side_in, side_out):
            dst[...] = src[...].astype(BF16)
        ws = weight_chunks()
        for r in range(0, bm, min(bm, ROW_SPLIT)):
            rs = slice(r, r + min(bm, ROW_SPLIT))
            sc = sc_ref[0] if sc_ref.shape[1] == 1 else sc_ref[0, rs]
            sh = sh_ref[0] if sh_ref.shape[1] == 1 else sh_ref[0, rs]
            h = (_rms(x_ref[rs, :], g_ref[...]) * (1.0 + sc) + sh).astype(BF16)
            h_scr[rs, :] = h
            for cs, w in zip(col_chunks, ws):
                proj_ref[rs, cs] = jax.nn.gelu(chunk(h, w)).astype(BF16)

    @pl.when(j == 1)
    def _():
        ws = weight_chunks()
        for r in range(0, bm, min(bm, ROW_SPLIT)):
            rs = slice(r, r + min(bm, ROW_SPLIT))
            h = h_scr[rs, :]
            gs = [jax.nn.gelu(chunk(h, w)) for w in ws]
            mean = sum(jnp.sum(g, axis=-1, keepdims=True) for g in gs) * (1.0 / COL_TILE)
            xcs = [g - mean for g in gs]
            var = sum(jnp.sum(xc * xc, axis=-1, keepdims=True) for xc in xcs) * (1.0 / COL_TILE)
            inv = lax.rsqrt(var + EPS)
            for cs, xc in zip(col_chunks, xcs):
                y = xc * inv * lng_ref[:, cs] + lnb_ref[:, cs]
                if va_ref is not None:
                    va_ref[rs, cs] = y
                proj_ref[rs, cs] = y.astype(BF16)

    def raw(f32_ref):
        h = h_scr[...]
        for cs, w in zip(col_chunks, weight_chunks()):
            acc = chunk(h, w)
            if f32_ref is not None:
                f32_ref[:, cs] = acc
            proj_ref[:, cs] = acc.astype(BF16)

    pl.when(j == 2)(lambda: raw(None))
    pl.when(j == 3)(lambda: raw(k_ref))
    pl.when(j == 4)(lambda: raw(v_ref))

    @pl.when(j >= 5)
    def _():
        h = h_scr[...]
        for cs, w in zip(col_chunks, weight_chunks()):
            proj_ref[:, cs] = _sigmoid(chunk(h, w)).astype(BF16)


def _slab_spec(a, steps, index_of):
    share = 1
    while (a.shape[0] * share) % (steps * BF16_ROWS):
        share *= 2
    assert steps % share == 0
    return pl.BlockSpec((a.shape[0] * share // steps, a.shape[1]),
                        lambda *grid: (index_of(*grid) // share, 0))


def _inproj(x, sc, sh, g_pre, w_in, ln_g, ln_b, bm, want_va, round_on_the_side=()):
    m, d = x.shape
    n_in = w_in.shape[1]
    tiles = m // bm
    tiles_per_group = tiles // sc.shape[0]
    r = sc.shape[1]
    mod_spec = pl.BlockSpec((1, r, d), lambda i, j: (i // tiles_per_group, 0, 0))
    row_spec = pl.BlockSpec((bm, COL_TILE), lambda i, j: (i, 0))
    w_spec = pl.BlockSpec((d, COL_TILE), lambda i, j: (0, j))
    n_f32 = 3 if want_va else 2
    round_w = w_in.dtype != BF16
    assert not round_w or tiles == 1, "the weight copy is written once per column tile"
    slabs = [_slab_spec(a, tiles, lambda i, j: i) for a in round_on_the_side]
    out = pl.pallas_call(
        functools.partial(_inproj_kernel, want_va=want_va, n_side=len(slabs)),
        grid=(tiles, n_in // COL_TILE),
        in_specs=[
            pl.BlockSpec((bm, d), lambda i, j: (i, 0)),
            mod_spec, mod_spec,
            pl.BlockSpec((1, d), lambda i, j: (0, 0)),
            w_spec,
            pl.BlockSpec((1, A_WIDTH), lambda i, j: (0, 0)),
            pl.BlockSpec((1, A_WIDTH), lambda i, j: (0, 0)),
        ] + slabs,
        out_specs=[pl.BlockSpec((bm, COL_TILE), lambda i, j: (i, j))] + [row_spec] * n_f32
        + [w_spec] * round_w + slabs,
        out_shape=[jax.ShapeDtypeStruct((m, n_in), BF16)]
        + [jax.ShapeDtypeStruct((m, COL_TILE), F32)] * n_f32
        + [jax.ShapeDtypeStruct(w_in.shape, BF16)] * round_w
        + [jax.ShapeDtypeStruct(a.shape, BF16) for a in round_on_the_side],
        scratch_shapes=[pltpu.VMEM((bm, d), BF16)],
        compiler_params=_params(("arbitrary", "arbitrary")),
    )(x, sc, sh, g_pre.reshape(1, d), w_in, ln_g.reshape(1, A_WIDTH), ln_b.reshape(1, A_WIDTH),
      *round_on_the_side)
    n_main = 1 + n_f32 + round_w
    return (*out[:n_main], tuple(out[n_main:]))


def _log_terms(z):
    soft = jnp.log(1.0 + jnp.exp(-jnp.abs(z)))
    log_beta = jnp.minimum(z, 0.0) - soft
    return log_beta, log_beta - z


def _strict_lower(n):
    row = lax.broadcasted_iota(jnp.int32, (n, n), 0)
    col = lax.broadcasted_iota(jnp.int32, (n, n), 1)
    return (row > col).astype(BF16)


MASKED_LOG = -1e30


def _attn_rows(q_ref, k_ref, v_ref, o_ref, bias, tri, causal, a, blk):
    n_blocks = a + 1
    n = n_blocks * blk
    q = q_ref[a * blk:(a + 1) * blk, :]
    s = lax.dot_general(q, k_ref[0:n, :], (((1,), (1,)), ((), ())), preferred_element_type=F32)
    yield
    z = s * (1.0 / math.sqrt(HEAD_DIM)) + bias
    log_beta, log_1m = _log_terms(z)
    lbs = [log_beta[:, j * blk:(j + 1) * blk] for j in range(n_blocks)]
    l1s = [log_1m[:, j * blk:(j + 1) * blk] for j in range(n_blocks)]
    lbs[-1] = jnp.where(causal, lbs[-1], MASKED_LOG)
    l1s[-1] = jnp.where(causal, l1s[-1], 0.0)
    stacked = jnp.concatenate(l1s, axis=0).astype(BF16)
    yield
    excl = jnp.dot(stacked, tri, preferred_element_type=F32)
    yield
    excl = [excl[j * blk:(j + 1) * blk] for j in range(n_blocks)]
    run = jnp.zeros((blk, 1), F32)
    ws = [None] * n_blocks
    for j in reversed(range(n_blocks)):
        ws[j] = jnp.exp(lbs[j] + excl[j] + run).astype(BF16)
        run = run + excl[j][:, :1] + l1s[j][:, :1]
    w = jnp.concatenate(ws, axis=1)
    yield
    out = jnp.dot(w, v_ref[0:n, :], preferred_element_type=F32)
    o_ref[a * blk:(a + 1) * blk, :] = out.astype(BF16)


def _sample_pages(q, bias, k_refs, v_refs, run_scr, acc_scr):
    page_cols = k_refs[0].shape[0]
    n_chunks = page_cols // LANE
    head = lax.broadcasted_iota(jnp.int32, (N_HEADS, page_cols), 0)
    col = lax.broadcasted_iota(jnp.int32, (N_HEADS, page_cols), 1)
    own = (col % N_HEADS) == head
    tri_ones = jnp.concatenate([_strict_lower(LANE), jnp.ones((LANE, LANE), BF16)], axis=1)
    scores = [lax.dot_general(q, k[...].astype(BF16), (((1,), (1,)), ((), ())),
                              preferred_element_type=F32) for k in k_refs]
    yield
    log_betas, stacked = [], []
    for s in scores:
        log_beta, log_1m = _log_terms(s * (1.0 / math.sqrt(HEAD_DIM)) + bias)
        log_1m = jnp.where(own, log_1m, 0.0)
        log_betas.append(log_beta)
        stacked += [log_1m[:, c * LANE:(c + 1) * LANE] for c in range(n_chunks)]
    stacked = jnp.concatenate(stacked, axis=0).astype(BF16)
    yield
    sums = jnp.dot(stacked, tri_ones, preferred_element_type=F32)
    yield
    run = run_scr[...]
    weights = []
    for r in range(len(k_refs)):
        later = [None] * n_chunks
        for c in reversed(range(n_chunks)):
            rows = sums[(r * n_chunks + c) * N_HEADS:(r * n_chunks + c + 1) * N_HEADS]
            later[c] = rows[:, :LANE] + run
            run = run + rows[:, LANE:]
        w = jnp.exp(log_betas[r] + jnp.concatenate(later, axis=1))
        weights.append(jnp.where(own, w, 0.0).astype(BF16))
    run_scr[...] = run
    yield
    acc = acc_scr[...]
    for w, v in zip(weights, v_refs):
        acc = acc + jnp.dot(w, v[...].astype(BF16), preferred_element_type=F32)
    acc_scr[...] = acc


def _interleave(generators, in_flight):
    waiting, active = list(generators), []
    while waiting or active:
        active = [g for g in active if next(g, "done") != "done"]
        if waiting and len(active) < in_flight:
            g = waiting.pop(0)
            next(g)
            active.append(g)


def _attn_kernel(pt_ref, bias_ref, q_ref, k_ref, v_ref, qs_ref, bias_col_ref, ck_ref, cv_ref,
                 o_ref, os_ref, k_buf, v_buf, sem, run_scr, acc_scr,
                 *, blk, groups, steps_per_seq, first_page, in_flight):
    pages_per_step = k_buf.shape[1]
    n_pages = pages_per_step * steps_per_seq
    group = pl.program_id(2)
    step = (pl.program_id(0) * pl.num_programs(1) + pl.program_id(1)) * groups + group
    n_steps = pl.num_programs(0) * pl.num_programs(1) * groups
    within = lax.rem(step, steps_per_seq)
    slot = lax.rem(step, 2)
    n_blocks = q_ref.shape[0] // blk

    def page_copies(of_step, to_slot):
        seq_id = lax.div(of_step, steps_per_seq)
        part = lax.rem(of_step, steps_per_seq)
        copies = []
        for r in range(pages_per_step):
            page = first_page + pt_ref[seq_id, n_pages - 1 - (part * pages_per_step + r)]
            copies.append((pltpu.make_async_copy(ck_ref.at[page], k_buf.at[to_slot, r], sem.at[to_slot, 0]),
                           pltpu.make_async_copy(cv_ref.at[page], v_buf.at[to_slot, r], sem.at[to_slot, 1])))
        return copies

    def start(copies):
        for k_copy, v_copy in copies:
            k_copy.start(priority=0)
            v_copy.start(priority=1)

    @pl.when(step == 0)
    def _():
        start(page_copies(step, slot))

    @pl.when(step + 1 < n_steps)
    def _():
        start(page_copies(step + 1, 1 - slot))

    for k_copy, v_copy in page_copies(step, slot):
        k_copy.wait()
        v_copy.wait()
    k_pages = [k_buf.at[slot, r] for r in range(pages_per_step)]
    v_pages = [v_buf.at[slot, r] for r in range(pages_per_step)]

    @pl.when(within == 0)
    def _():
        run_scr[...] = jnp.zeros_like(run_scr)
        acc_scr[...] = jnp.zeros_like(acc_scr)

    bias = bias_ref[pl.program_id(1)]
    tri = _strict_lower(blk)
    row = lax.broadcasted_iota(jnp.int32, (blk, blk), 0)
    col = lax.broadcasted_iota(jnp.int32, (blk, blk), 1)
    causal = col < row

    def body(p):
        mine = [a for a in range(n_blocks) if a % (2 * groups) in (p, 2 * groups - 1 - p)]
        _interleave(
            [_sample_pages(qs_ref[0], bias_col_ref[...], k_pages, v_pages, run_scr, acc_scr)]
            + [_attn_rows(q_ref, k_ref, v_ref, o_ref, bias, tri, causal, a, blk) for a in mine],
            in_flight)

    for p in range(groups):
        pl.when(group == p)(functools.partial(body, p))

    @pl.when(within == steps_per_seq - 1)
    def _():
        os_ref[0] = acc_scr[...].astype(BF16)


def _attn(proj, sb_bias, batch, seq, q_s, cache_k, cache_v, layer, page_table, blk, groups, in_flight):
    db = q_s.shape[0]
    depth, n_pool, page, heads, d = cache_k.shape
    n_pages = page_table.shape[1]
    steps = batch * heads * groups
    assert (seq // blk) % (2 * groups) == 0 and (db * n_pages) % steps == 0
    pages_per_step = db * n_pages // steps
    assert n_pages % pages_per_step == 0
    steps_per_seq = n_pages // pages_per_step
    ck = cache_k.reshape(depth * n_pool, page * heads, d)
    cv = cache_v.reshape(depth * n_pool, page * heads, d)
    q_col, k_col, v_col = (2 * A_WIDTH) // LANE, (2 * A_WIDTH + SB_WIDTH) // LANE, (2 * A_WIDTH + 2 * SB_WIDTH) // LANE

    def head_spec(col0):
        return pl.BlockSpec((seq, HEAD_DIM), lambda b, h, p, pt, bias: (b, col0 + h))

    def seq_spec():
        return pl.BlockSpec((1, heads, d), lambda b, h, p, pt, bias:
                            (lax.div((b * heads + h) * groups + p, steps_per_seq), 0, 0))

    page_buf = pltpu.VMEM((2, pages_per_step, page * heads, d), F32)
    grid_spec = pltpu.PrefetchScalarGridSpec(
        num_scalar_prefetch=2,
        grid=(batch, heads, groups),
        in_specs=[head_spec(q_col), head_spec(k_col), head_spec(v_col), seq_spec(),
                  pl.BlockSpec((heads, 1), lambda b, h, p, pt, bias: (0, 0)),
                  pl.BlockSpec(memory_space=pl.ANY), pl.BlockSpec(memory_space=pl.ANY)],
        out_specs=[head_spec(0), seq_spec()],
        scratch_shapes=[page_buf, page_buf, pltpu.SemaphoreType.DMA((2, 2)),
                        pltpu.VMEM((heads, LANE), F32), pltpu.VMEM((heads, d), F32)],
    )
    return pl.pallas_call(
        functools.partial(_attn_kernel, blk=blk, groups=groups, steps_per_seq=steps_per_seq,
                          first_page=layer * n_pool, in_flight=in_flight),
        grid_spec=grid_spec,
        out_shape=[jax.ShapeDtypeStruct((batch * seq, SB_WIDTH), BF16),
                   jax.ShapeDtypeStruct((db, heads, d), BF16)],
        compiler_params=_params(("arbitrary", "arbitrary", "arbitrary")),
    )(page_table, sb_bias, proj, proj, proj, q_s, sb_bias.reshape(heads, 1), ck, cv)


def _mix_kernel(u_ref, va_ref, b_ref, ga0_ref, ga1_ref, gb0_ref, gb1_ref, x_ref, gt_ref, gpost_ref,
                ws_ref, bs_ref, wa_ref, wb_ref, wo_ref, *rest, chunked):
    n_side = len(rest) // 2
    o_ref = rest[n_side]
    for src, dst in zip(rest[:n_side], rest[n_side + 1:]):
        dst[...] = src[...].astype(BF16)
    bm = u_ref.shape[0]
    u = u_ref[...].astype(F32)
    if chunked:
        row = lax.broadcasted_iota(jnp.int32, (CHUNK, CHUNK), 0)
        col = lax.broadcasted_iota(jnp.int32, (CHUNK, CHUNK), 1)
        keep = col <= row
        cols = []
        for g in range(A_GROUPS):
            w = jnp.where(keep, ws_ref[g], 0.0).astype(BF16)
            b = bs_ref[g]
            rows = []
            for c in range(bm // CHUNK):
                v = va_ref[c * CHUNK:(c + 1) * CHUNK, g * CHUNK:(g + 1) * CHUNK]
                rows.append(jnp.dot(w, v, preferred_element_type=F32) + b)
            cols.append(jnp.concatenate(rows, axis=0))
        s = jnp.concatenate(cols, axis=1)
    else:
        s = va_ref[...].astype(F32) * ws_ref[...] + bs_ref[...]
    a = (u * s).astype(BF16)
    p = jnp.dot(a, wa_ref[...], preferred_element_type=F32)
    q = jnp.dot(b_ref[...], wb_ref[...], preferred_element_type=F32)
    half = p.shape[1] // 2
    merged = jnp.concatenate([
        ga0_ref[...].astype(F32) * p[:, :half] + gb0_ref[...].astype(F32) * q[:, :half],
        ga1_ref[...].astype(F32) * p[:, half:] + gb1_ref[...].astype(F32) * q[:, half:],
    ], axis=1).astype(BF16)
    mix = jnp.dot(merged, wo_ref[...], preferred_element_type=F32)
    o_ref[...] = x_ref[...] + gt_ref[0] * _rms(mix, gpost_ref[...])


def _mix(proj, b_out, x, gt, g_post, ws, bs, wa, wb, wo, bm, chunked, round_on_the_side=()):
    m, d = x.shape
    steps = m // bm
    tiles_per_group = steps // gt.shape[0]
    r = gt.shape[1]

    def col_spec(c):
        return pl.BlockSpec((bm, COL_TILE), lambda i: (i, c))

    def whole(a):
        return pl.BlockSpec(a.shape, lambda i: (0,) * a.ndim)

    slabs = [_slab_spec(a, steps, lambda i: i) for a in round_on_the_side]
    out = pl.pallas_call(
        functools.partial(_mix_kernel, chunked=chunked),
        grid=(steps,),
        in_specs=[
            col_spec(0), col_spec(1),
            pl.BlockSpec((bm, SB_WIDTH), lambda i: (i, 0)),
            col_spec(5), col_spec(6), col_spec(7), col_spec(8),
            pl.BlockSpec((bm, d), lambda i: (i, 0)),
            pl.BlockSpec((1, r, d), lambda i: (i // tiles_per_group, 0, 0)),
            pl.BlockSpec((1, d), lambda i: (0, 0)),
            whole(ws), whole(bs), whole(wa), whole(wb), whole(wo),
        ] + slabs,
        out_specs=[pl.BlockSpec((bm, d), lambda i: (i, 0))] + slabs,
        out_shape=[jax.ShapeDtypeStruct((m, d), F32)]
        + [jax.ShapeDtypeStruct(a.shape, BF16) for a in round_on_the_side],
        compiler_params=_params(("arbitrary",)),
    )(proj, proj, b_out, proj, proj, proj, proj, x, gt, g_post.reshape(1, d), ws, bs, wa, wb, wo,
      *round_on_the_side)
    return out[0], out[1:]


def _ffn_kernel(x_ref, sc_ref, sh_ref, gt_ref, gpre_ref, gpost_ref, wg_ref, wu_ref, wo_ref, o_ref,
                h_scr, acc_scr):
    j = pl.program_id(1)
    last = pl.num_programs(1) - 1
    bm = x_ref.shape[0]
    row_blocks = [slice(r, r + min(bm, ROW_SPLIT)) for r in range(0, bm, min(bm, ROW_SPLIT))]

    def rows_of(ref, rs):
        return ref[0] if ref.shape[1] == 1 else ref[0, rs]

    def weights():
        return wg_ref[...], wu_ref[...], wo_ref[...]

    def partial_out(h, w):
        gate = jnp.dot(h, w[0], preferred_element_type=F32)
        up = jnp.dot(h, w[1], preferred_element_type=F32)
        act = (gate * _sigmoid(gate) * up).astype(BF16)
        return jnp.dot(act, w[2], preferred_element_type=F32)

    @pl.when(j == 0)
    def _():
        w = weights()
        for rs in row_blocks:
            h = (_rms(x_ref[rs, :], gpre_ref[...]) * (1.0 + rows_of(sc_ref, rs))
                 + rows_of(sh_ref, rs)).astype(BF16)
            h_scr[rs, :] = h
            acc_scr[rs, :] = partial_out(h, w)

    @pl.when(jnp.logical_and(j > 0, j < last))
    def _():
        acc_scr[...] += partial_out(h_scr[...], weights())

    @pl.when(j == last)
    def _():
        w = weights()
        for rs in row_blocks:
            f = acc_scr[rs, :] + partial_out(h_scr[rs, :], w)
            o_ref[rs, :] = x_ref[rs, :] + rows_of(gt_ref, rs) * _rms(f, gpost_ref[...])


def _ffn(x, sc, sh, gt, g_pre, g_post, w_ffn_in, w_ffn_out, bm, tf):
    m, d = x.shape
    d_ff = w_ffn_out.shape[0]
    nf = d_ff // tf
    assert nf >= 2, "the kernel treats the first and last d_ff steps separately"
    tiles_per_group = m // bm // sc.shape[0]
    r = sc.shape[1]
    mod_spec = pl.BlockSpec((1, r, d), lambda i, j: (i // tiles_per_group, 0, 0))
    vec_spec = pl.BlockSpec((1, d), lambda i, j: (0, 0))
    return pl.pallas_call(
        _ffn_kernel,
        grid=(m // bm, nf),
        in_specs=[
            pl.BlockSpec((bm, d), lambda i, j: (i, 0)),
            mod_spec, mod_spec, mod_spec, vec_spec, vec_spec,
            pl.BlockSpec((d, tf), lambda i, j: (0, j)),
            pl.BlockSpec((d, tf), lambda i, j: (0, nf + j)),
            pl.BlockSpec((tf, d), lambda i, j: (j, 0)),
        ],
        out_specs=pl.BlockSpec((bm, d), lambda i, j: (i, 0)),
        out_shape=jax.ShapeDtypeStruct((m, d), F32),
        scratch_shapes=[pltpu.VMEM((bm, d), BF16), pltpu.VMEM((bm, d), F32)],
        compiler_params=_params(("arbitrary", "arbitrary")),
    )(x, sc, sh, gt, g_pre.reshape(1, d), g_post.reshape(1, d), w_ffn_in, w_ffn_in, w_ffn_out)


def kernel(x_prompt, x_sample, c_prompt, c_sample, cache_k, cache_v, page_table, w_ada, b_ada, g_pre_mix, g_post_mix, w_in, ln_v_g, ln_v_b, w_s, b_s, sb_bias, w_branch_a, w_branch_b, w_out, g_pre_ffn, g_post_ffn, w_ffn_in, w_ffn_out):
    batch, seq, d = x_prompt.shape
    db, dseq, _ = x_sample.shape
    depth = w_ada.shape[0]
    assert dseq == 1, "the sample group decodes one token per sequence"
    mp, ms = batch * seq, db * dseq
    yp = x_prompt.reshape(mp, d)
    ys = x_sample.reshape(ms, d)
    c_all = jnp.concatenate([c_prompt, c_sample], axis=0)
    outs = {name: [] for name in ("kp", "vp", "ks", "vs", "cv")}

    for l in range(depth):
        mods = _ada(c_all, w_ada[l], b_ada[l])
        mods_p = [m.reshape(batch, 1, d) for m in jnp.split(mods[:batch], 6, axis=-1)]
        mods_s = [m.reshape(1, db, d) for m in jnp.split(mods[batch:], 6, axis=-1)]

        sh1, sc1, gt1, sh2, sc2, gt2 = mods_p
        sh1_s, sc1_s, gt1_s, sh2_s, sc2_s, gt2_s = mods_s
        proj_s, k_s, v_s, cv_s, w_in_l, _ = _inproj(
            ys, sc1_s, sh1_s, g_pre_mix[l], w_in[l], ln_v_g[l], ln_v_b[l], bm=ms, want_va=True)
        proj, kp, vp, (wa_l, wb_l, wo_l) = _inproj(
            yp, sc1, sh1, g_pre_mix[l], w_in_l, ln_v_g[l], ln_v_b[l], bm=INPROJ_ROWS, want_va=False,
            round_on_the_side=(w_branch_a[l], w_branch_b[l], w_out[l]))
        q_s = proj_s[:, 2 * A_WIDTH:2 * A_WIDTH + SB_WIDTH].reshape(ms, N_HEADS, HEAD_DIM)
        b_out, b_out_s = _attn(proj, sb_bias[l], batch, seq, q_s, cache_k, cache_v, l, page_table,
                               blk=ATTN_BLOCK, groups=ATTN_GROUPS, in_flight=ATTN_IN_FLIGHT)

        x1, (wfi_l, wfo_l) = _mix(proj, b_out, yp, gt1, g_post_mix[l], w_s[l],
                                  b_s[l].reshape(A_GROUPS, CHUNK, 1), wa_l, wb_l, wo_l, bm=MIX_ROWS,
                                  chunked=True, round_on_the_side=(w_ffn_in[l], w_ffn_out[l]))
        yp = _ffn(x1, sc2, sh2, gt2, g_pre_ffn[l], g_post_ffn[l], wfi_l, wfo_l,
                  bm=FFN_ROWS, tf=FFN_COLS)
        outs["kp"].append(kp.reshape(batch, seq, N_HEADS, HEAD_DIM))
        outs["vp"].append(vp.reshape(batch, seq, N_HEADS, HEAD_DIM))

        ws_row = jnp.repeat(w_s[l][:, 0, 0], CHUNK).reshape(1, A_WIDTH)
        bs_row = jnp.repeat(b_s[l][:, 0], CHUNK).reshape(1, A_WIDTH)
        x1, _ = _mix(proj_s, b_out_s.reshape(ms, SB_WIDTH), ys, gt1_s, g_post_mix[l], ws_row, bs_row,
                     wa_l, wb_l, wo_l, bm=ms, chunked=False)
        ys = _ffn(x1, sc2_s, sh2_s, gt2_s, g_pre_ffn[l], g_post_ffn[l], wfi_l, wfo_l,
                  bm=ms, tf=FFN_COLS_SAMPLE)
        outs["ks"].append(k_s.reshape(db, dseq, N_HEADS, HEAD_DIM))
        outs["vs"].append(v_s.reshape(db, dseq, N_HEADS, HEAD_DIM))
        outs["cv"].append(cv_s.reshape(db, dseq, A_WIDTH))

    return (yp.reshape(batch, seq, d), ys.reshape(db, dseq, d),
            jnp.stack(outs["kp"]), jnp.stack(outs["vp"]),
            jnp.stack(outs["ks"]), jnp.stack(outs["vs"]), jnp.stack(outs["cv"]))
```
